```python
import jax, jax.numpy as jnp
from jax import lax
import numpy as np

D_MODEL = 2048
BATCH = 2
SEQ = 16384
DEPTH = 1

PLE_DIM = 256
MIX_WIDTH = D_MODEL
HGRN_WIDTH = MIX_WIDTH // 2
SGU_WIDTH = MIX_WIDTH - HGRN_WIDTH
HGRN_HEAD_DIM = 128
HGRN_HEADS = HGRN_WIDTH // HGRN_HEAD_DIM
HGRN_CHUNK = 64
SGU_CHUNK = 128
SGU_HEADS = 8
SGU_HEAD_DIM = SGU_WIDTH // SGU_HEADS
FFN_HIDDEN = -(-(8 * D_MODEL) // (3 * 256)) * 256
IN_COLS = 5 * HGRN_WIDTH + 2 * SGU_WIDTH
EPS = 1e-6

kernel_name = "hymba_hgrn2_gmlp_encoder_block"


def _rmsnorm(x, g):
    xf = x.astype(jnp.float32)
    y = xf * lax.rsqrt(jnp.mean(xf * xf, axis=-1, keepdims=True) + EPS)
    return (y * g.astype(jnp.float32)).astype(x.dtype)


def _layernorm(x, g, b):
    xf = x.astype(jnp.float32)
    mu = jnp.mean(xf, axis=-1, keepdims=True)
    xc = xf - mu
    y = xc * lax.rsqrt(jnp.mean(xc * xc, axis=-1, keepdims=True) + EPS)
    return (y * g.astype(jnp.float32) + b.astype(jnp.float32)).astype(x.dtype)


def _gla_chunkwise(q, k, v, log_f):
    B, H, L, K = q.shape
    V = v.shape[-1]
    n = L // HGRN_CHUNK

    def to_chunks(t):
        return t.reshape(B, H, n, HGRN_CHUNK, t.shape[-1]).transpose(2, 0, 1, 3, 4)

    mask = jnp.tril(jnp.ones((HGRN_CHUNK, HGRN_CHUNK), dtype=bool))[:, :, None]

    def step(S, inp):
        qi, ki, vi, gi = inp
        b = jnp.cumsum(gi, axis=2)
        diff = b[:, :, :, None, :] - b[:, :, None, :, :]
        decay = jnp.exp(jnp.where(mask, diff, -jnp.inf))
        scores = jnp.einsum('bhtk,bhsk,bhtsk->bhts', qi, ki, decay)
        o = (jnp.einsum('bhts,bhsv->bhtv', scores, vi)
             + jnp.einsum('bhtk,bhkv->bhtv', qi * jnp.exp(b), S))
        b_last = b[:, :, -1:, :]
        S = (jnp.exp(b_last[:, :, 0, :])[..., None] * S
             + jnp.einsum('bhsk,bhsv->bhkv', ki * jnp.exp(b_last - b), vi))
        return S, o

    S0 = jnp.zeros((B, H, K, V), jnp.float32)
    _, o = lax.scan(step, S0, (to_chunks(q), to_chunks(k), to_chunks(v), to_chunks(log_f)))
    return o.transpose(1, 2, 0, 3, 4).reshape(B, H, L, V)


def _hgrn2_bidir(q_raw, zf, zb, i_raw, g_raw, lb_f, lb_b, onorm_g):
    B, L, _ = q_raw.shape

    def heads(t):
        return t.astype(jnp.float32).reshape(B, L, HGRN_HEADS, HGRN_HEAD_DIM).transpose(0, 2, 1, 3)

    q = jax.nn.silu(heads(q_raw))
    v = heads(i_raw)

    def gates(z, lb):
        z = heads(z)
        lb = lb.astype(jnp.float32).reshape(HGRN_HEADS, 1, HGRN_HEAD_DIM)
        log_f = jnp.log(lb + (1.0 - lb) * jax.nn.sigmoid(z))
        k = (1.0 - lb) * jax.nn.sigmoid(-z)
        return k, log_f

    k_f, lf_f = gates(zf, lb_f)
    k_b, lf_b = gates(zb, lb_b)
    flip = lambda t: jnp.flip(t, axis=2)
    o_fwd = _gla_chunkwise(q, k_f, v, lf_f)
    o_bwd = flip(_gla_chunkwise(flip(q), flip(k_b), flip(v), flip(lf_b)))
    o = o_fwd + o_bwd
    o = o * lax.rsqrt(jnp.mean(o * o, axis=-1, keepdims=True) + EPS)
    o = o * onorm_g.astype(jnp.float32) * jax.nn.silu(heads(g_raw))
    return o.transpose(0, 2, 1, 3).reshape(B, L, HGRN_WIDTH).astype(q_raw.dtype)


def _sgu(u_raw, v_raw, ln_g, ln_b, w_s, b_s):
    B, L, _ = u_raw.shape
    u = jax.nn.gelu(u_raw, approximate=False)
    v = _layernorm(jax.nn.gelu(v_raw, approximate=False), ln_g, ln_b)
    vc = v.reshape(B, L // SGU_CHUNK, SGU_CHUNK, SGU_HEADS, SGU_HEAD_DIM)
    mixed = jnp.einsum('gts,bnsgd->bntgd', w_s, vc) + b_s.T[:, :, None]
    return u * mixed.reshape(B, L, SGU_WIDTH)


def setup_inputs(seed: int = 0) -> dict:
    key = jax.random.key(seed)
    ks = jax.random.split(key, 24)
    nrm = lambda k, shape, s: jax.random.normal(k, shape, jnp.float32) * s
    gain = lambda k, shape: 1.0 + nrm(k, shape, 0.02)
    return {
        "x": nrm(ks[0], (BATCH, SEQ, D_MODEL), 1.0),
        "p": nrm(ks[1], (DEPTH, BATCH, SEQ, PLE_DIM), 1.0),
        "norm_mix_g": gain(ks[2], (DEPTH, D_MODEL)),
        "w_in": nrm(ks[3], (DEPTH, D_MODEL, IN_COLS), D_MODEL ** -0.5),
        "lb_fwd_logits": nrm(ks[4], (DEPTH + 1, HGRN_WIDTH), 0.1),
        "lb_bwd_logits": nrm(ks[5], (DEPTH + 1, HGRN_WIDTH), 0.1),
        "hgrn_onorm_g": gain(ks[6], (DEPTH, HGRN_HEAD_DIM)),
        "sgu_ln_g": gain(ks[7], (DEPTH, SGU_WIDTH)),
        "sgu_ln_b": nrm(ks[8], (DEPTH, SGU_WIDTH), 0.02),
        "sgu_w": nrm(ks[9], (DEPTH, SGU_HEADS, SGU_CHUNK, SGU_CHUNK), SGU_CHUNK ** -0.5),
        "sgu_b": gain(ks[10], (DEPTH, SGU_HEADS, SGU_CHUNK)),
        "sgu_onorm_g": gain(ks[11], (DEPTH, SGU_WIDTH)),
        "w_out": nrm(ks[12], (DEPTH, MIX_WIDTH, D_MODEL), MIX_WIDTH ** -0.5),
        "norm_ffn_g": gain(ks[13], (DEPTH, D_MODEL)),
        "w_gate": nrm(ks[14], (DEPTH, D_MODEL, FFN_HIDDEN), D_MODEL ** -0.5),
        "w_up": nrm(ks[15], (DEPTH, D_MODEL, FFN_HIDDEN), D_MODEL ** -0.5),
        "w_down": nrm(ks[16], (DEPTH, FFN_HIDDEN, D_MODEL), FFN_HIDDEN ** -0.5),
        "norm_ple_g": gain(ks[17], (DEPTH, D_MODEL)),
        "w_ple_gate": nrm(ks[18], (DEPTH, D_MODEL, D_MODEL), D_MODEL ** -0.5),
        "w_ple_proj": nrm(ks[19], (DEPTH, PLE_DIM, D_MODEL), PLE_DIM ** -0.5),
        "final_norm_g": gain(ks[20], (D_MODEL,)),
    }


def reference(x, p, norm_mix_g, w_in, lb_fwd_logits, lb_bwd_logits, hgrn_onorm_g,
              sgu_ln_g, sgu_ln_b, sgu_w, sgu_b, sgu_onorm_g, w_out, norm_ffn_g,
              w_gate, w_up, w_down, norm_ple_g, w_ple_gate, w_ple_proj, final_norm_g):
    lb_f_all = jnp.cumsum(jax.nn.softmax(lb_fwd_logits.astype(jnp.float32), axis=0), axis=0)
    lb_b_all = jnp.cumsum(jax.nn.softmax(lb_bwd_logits.astype(jnp.float32), axis=0), axis=0)
    W, S = HGRN_WIDTH, SGU_WIDTH
    split_at = [W, 2 * W, 3 * W, 4 * W, 5 * W, 5 * W + S]
    h = x
    for layer in range(DEPTH):
        xn = _rmsnorm(h, norm_mix_g[layer])
        proj = xn @ w_in[layer]
        q_r, zf, zb, i_r, g_r, u_r, v_r = jnp.split(proj, split_at, axis=-1)
        a = _hgrn2_bidir(q_r, zf, zb, i_r, g_r, lb_f_all[layer], lb_b_all[layer],
                         hgrn_onorm_g[layer])
        s = _sgu(u_r, v_r, sgu_ln_g[layer], sgu_ln_b[layer], sgu_w[layer], sgu_b[layer])
        s = _rmsnorm(s, sgu_onorm_g[layer])
        h = h + jnp.concatenate([a, s], axis=-1) @ w_out[layer]
        hn = _rmsnorm(h, norm_ffn_g[layer])
        h = h + (jax.nn.silu(hn @ w_gate[layer]) * (hn @ w_up[layer])) @ w_down[layer]
        hp = _rmsnorm(h, norm_ple_g[layer])
        h = h + jax.nn.sigmoid(hp @ w_ple_gate[layer]) * (p[layer] @ w_ple_proj[layer])
    return _rmsnorm(h, final_norm_g)
```

```python
import functools

import jax
import jax.numpy as jnp
from jax import lax
from jax.experimental import pallas as pl
from jax.experimental.pallas import tpu as pltpu

F32 = jnp.float32
BF16 = jnp.bfloat16
EPS = 1e-6

HEADS = 8
HEAD_DIM = 128
SCAN_CHUNK = 64
SGU_CHUNK = 128
LEVELS = (1, 2, 4, 8, 16, 32)
VMEM_LIMIT = 60 * 1024 * 1024


def _dot(a, b):
    return jnp.dot(a, b, preferred_element_type=F32)


def _dot_nt(a, b):
    return lax.dot_general(a, b, (((1,), (1,)), ((), ())), preferred_element_type=F32)


def _dot_tn(a, b):
    return lax.dot_general(a, b, (((0,), (0,)), ((), ())), preferred_element_type=F32)


def _sigmoid(x):
    return 1.0 / (1.0 + jnp.exp(-x))


def _rms(x, g):
    return x * lax.rsqrt(jnp.mean(x * x, axis=-1, keepdims=True) + EPS) * g


def _resident(shape):
    nd = len(shape)
    return pl.BlockSpec(shape, lambda *_: (0,) * nd, pipeline_mode=pl.Buffered(1))


def _inproj_kernel(x_ref, gmix_ref, win_ref, lbf_ref, lbb_ref, lng_ref, lnb_ref,
                   sw_ref, sb_ref, sog_ref,
                   q_ref, kf_ref, kb_ref, lff_ref, lfb_ref, v_ref, g_ref, s_ref):
    width = HEADS * HEAD_DIM
    tm = x_ref.shape[0]
    xb = _rms(x_ref[...], gmix_ref[...]).astype(BF16)

    def proj(j):
        return _dot(xb, win_ref[:, j * width:(j + 1) * width])

    def put_heads(ref, val):
        for h in range(HEADS):
            ref[h] = val[:, h * HEAD_DIM:(h + 1) * HEAD_DIM].astype(ref.dtype)

    q = proj(0)
    put_heads(q_ref, q * _sigmoid(q))

    def gates(z, lb, k_ref, lf_ref):
        sig = _sigmoid(z)
        put_heads(lf_ref, jnp.log(lb + (1.0 - lb) * sig))
        put_heads(k_ref, (1.0 - lb) * (1.0 - sig))

    gates(proj(1), lbf_ref[...], kf_ref, lff_ref)
    gates(proj(2), lbb_ref[...], kb_ref, lfb_ref)
    put_heads(v_ref, proj(3))
    g = proj(4)
    put_heads(g_ref, g * _sigmoid(g))

    def gelu(t):
        return 0.5 * t * (1.0 + lax.erf(t * (2.0 ** -0.5)))

    u = gelu(proj(5))
    v = gelu(proj(6))
    mu = jnp.mean(v, axis=-1, keepdims=True)
    vc = v - mu
    v = vc * lax.rsqrt(jnp.mean(vc * vc, axis=-1, keepdims=True) + EPS)
    v = (v * lng_ref[...] + lnb_ref[...]).astype(BF16)
    for c in range(tm // SGU_CHUNK):
        rows = slice(c * SGU_CHUNK, (c + 1) * SGU_CHUNK)
        mixed = [
            _dot(sw_ref[gi], v[rows, gi * HEAD_DIM:(gi + 1) * HEAD_DIM]) + sb_ref[gi]
            for gi in range(HEADS)
        ]
        s = u[rows] * jnp.concatenate(mixed, axis=1)
        s_ref[rows, :] = _rms(s, sog_ref[...]).astype(s_ref.dtype)


def _inproj_call(x2, gmix, win, lbf, lbb, lng, lnb, sw, sb, sog, *, batch, seq, tm):
    tokens, d_model = x2.shape
    width = HEADS * HEAD_DIM
    nlb = seq // tm
    row = lambda a: _resident(a.shape)
    head_spec = pl.BlockSpec((None, HEADS, tm, HEAD_DIM), lambda i: (i // nlb, 0, i % nlb, 0))
    head_shape = lambda dt: jax.ShapeDtypeStruct((batch, HEADS, seq, HEAD_DIM), dt)
    return pl.pallas_call(
        _inproj_kernel,
        grid=(tokens // tm,),
        in_specs=[pl.BlockSpec((tm, d_model), lambda i: (i, 0)),
                  row(gmix), row(win), row(lbf), row(lbb), row(lng), row(lnb),
                  row(sw), row(sb), row(sog)],
        out_specs=[head_spec] * 7 + [pl.BlockSpec((tm, width), lambda i: (i, 0))],
        out_shape=[head_shape(BF16), head_shape(BF16), head_shape(BF16),
                   head_shape(F32), head_shape(F32), head_shape(BF16), head_shape(BF16),
                   jax.ShapeDtypeStruct((tokens, width), BF16)],
        compiler_params=pltpu.CompilerParams(
            dimension_semantics=("arbitrary",), vmem_limit_bytes=VMEM_LIMIT),
        name="inproj",
    )(x2, gmix, win, lbf, lbb, lng, lnb, sw, sb, sog)


def _cumsum_rows(x, row):
    s = 1
    while s < SCAN_CHUNK:
        x = x + jnp.where(row >= s, pltpu.roll(x, s, axis=0), 0.0)
        s *= 2
    return x


def _block_ref(b, period, r0, row):
    if period >= 8:
        b3 = b.reshape(SCAN_CHUNK // period, period, HEAD_DIM)
        ref = jnp.broadcast_to(b3[:, r0:r0 + 1, :], b3.shape)
        return ref.reshape(SCAN_CHUNK, HEAD_DIM)
    pos = row % period
    out = b
    for r in range(period):
        if r != r0:
            out = jnp.where(pos == r, pltpu.roll(b, (r - r0) % SCAN_CHUNK, axis=0), out)
    return out


def _intra_scores(q, kf, kb, bf, bb, row, pair_xor):
    a = jnp.zeros((SCAN_CHUNK, SCAN_CHUNK), F32)
    for h in LEVELS:
        late = (row % (2 * h)) >= h
        df = jnp.abs(bf - _block_ref(bf, 2 * h, h - 1, row))
        db = jnp.abs(bb - _block_ref(bb, 2 * h, h, row))
        lhs = q * jnp.exp(-jnp.where(late, df, db))
        rhs = jnp.where(late, kb, kf) * jnp.exp(-jnp.where(late, db, df))
        g = _dot_nt(lhs.astype(BF16), rhs.astype(BF16))
        a = jnp.where(pair_xor >= h, g, a)
    return a


def _bwd_scan_kernel(kb_ref, lfb_ref, v_ref, hist_ref, st_ref):
    nchunk = kb_ref.shape[0] // SCAN_CHUNK

    @pl.when(pl.program_id(1) == 0)
    def _():
        st_ref[...] = jnp.zeros_like(st_ref)

    row = lax.broadcasted_iota(jnp.int32, (SCAN_CHUNK, HEAD_DIM), 0)

    def body(ci, carry):
        c = nchunk - 1 - ci
        rows = pl.ds(pl.multiple_of(c * SCAN_CHUNK, SCAN_CHUNK), SCAN_CHUNK)
        lf = lfb_ref[rows, :]
        st = st_ref[...]
        hist_ref[c] = st.astype(hist_ref.dtype)
        incl = _cumsum_rows(lf, row)
        tot = incl[SCAN_CHUNK - 1:SCAN_CHUNK, :]
        kt = (kb_ref[rows, :].astype(F32) * jnp.exp(incl - lf)).astype(BF16)
        st_ref[...] = st * jnp.exp(tot) + _dot_tn(v_ref[rows, :], kt)
        return carry

    lax.fori_loop(0, nchunk, body, 0)


def _bwd_scan_call(kb, lfb, v, *, tb):
    bh, seq, _ = kb.shape
    nblk = seq // tb
    spec = pl.BlockSpec((None, tb, HEAD_DIM), lambda b, i: (b, nblk - 1 - i, 0))
    return pl.pallas_call(
        _bwd_scan_kernel,
        grid=(bh, nblk),
        in_specs=[spec, spec, spec],
        out_specs=pl.BlockSpec((None, tb // SCAN_CHUNK, HEAD_DIM, HEAD_DIM),
                               lambda b, i: (b, nblk - 1 - i, 0, 0)),
        out_shape=jax.ShapeDtypeStruct((bh, seq // SCAN_CHUNK, HEAD_DIM, HEAD_DIM), BF16),
        scratch_shapes=[pltpu.VMEM((HEAD_DIM, HEAD_DIM), F32)],
        compiler_params=pltpu.CompilerParams(
            dimension_semantics=("arbitrary", "arbitrary"), vmem_limit_bytes=VMEM_LIMIT),
        name="hgrn_bwd_scan",
    )(kb, lfb, v)


def _fwd_kernel(q_ref, kf_ref, kb_ref, lff_ref, lfb_ref, v_ref, g_ref, hist_ref, og_ref,
                a_ref, st_ref):
    nchunk = q_ref.shape[0] // SCAN_CHUNK

    @pl.when(pl.program_id(1) == 0)
    def _():
        st_ref[...] = jnp.zeros_like(st_ref)

    row = lax.broadcasted_iota(jnp.int32, (SCAN_CHUNK, HEAD_DIM), 0)
    pair_xor = (lax.broadcasted_iota(jnp.int32, (SCAN_CHUNK, SCAN_CHUNK), 0)
                ^ lax.broadcasted_iota(jnp.int32, (SCAN_CHUNK, SCAN_CHUNK), 1))

    def body(c, carry):
        rows = pl.ds(pl.multiple_of(c * SCAN_CHUNK, SCAN_CHUNK), SCAN_CHUNK)
        q = q_ref[rows, :].astype(F32)
        kf = kf_ref[rows, :].astype(F32)
        kb = kb_ref[rows, :].astype(F32)
        lff = lff_ref[rows, :]
        lfb = lfb_ref[rows, :]
        v = v_ref[rows, :]
        vf = v.astype(F32)

        bf = _cumsum_rows(lff, row)
        incl_b = _cumsum_rows(lfb, row)
        bb = incl_b - lfb
        totf = bf[SCAN_CHUNK - 1:SCAN_CHUNK, :]
        totb = incl_b[SCAN_CHUNK - 1:SCAN_CHUNK, :]

        a = _intra_scores(q, kf, kb, bf, bb, row, pair_xor)
        diag = jnp.sum(q * (kf + kb), axis=-1, keepdims=True)
        o = _dot(a.astype(BF16), v) + diag * vf

        st = st_ref[...]
        q_in = jnp.concatenate([q * jnp.exp(bf), q * jnp.exp(totb - bb)], axis=1)
        s_in = jnp.concatenate([st.astype(BF16), hist_ref[c]], axis=1)
        o = o + _dot_nt(q_in.astype(BF16), s_in)

        kt = (kf * jnp.exp(totf - bf)).astype(BF16)
        st_ref[...] = st * jnp.exp(totf) + _dot_tn(v, kt)

        o = _rms(o, og_ref[...]) * g_ref[rows, :].astype(F32)
        a_ref[rows, :] = o.astype(a_ref.dtype)
        return carry

    lax.fori_loop(0, nchunk, body, 0)


def _fwd_call(q, kf, kb, lff, lfb, v, g, hist, og, *, tb):
    bh, seq, _ = q.shape
    spec = pl.BlockSpec((None, tb, HEAD_DIM), lambda b, i: (b, i, 0))
    return pl.pallas_call(
        _fwd_kernel,
        grid=(bh, seq // tb),
        in_specs=[spec] * 7 + [
            pl.BlockSpec((None, tb // SCAN_CHUNK, HEAD_DIM, HEAD_DIM), lambda b, i: (b, i, 0, 0)),
            _resident(og.shape)],
        out_specs=spec,
        out_shape=jax.ShapeDtypeStruct((bh, seq, HEAD_DIM), BF16),
        scratch_shapes=[pltpu.VMEM((HEAD_DIM, HEAD_DIM), F32)],
        compiler_params=pltpu.CompilerParams(
            dimension_semantics=("arbitrary", "arbitrary"), vmem_limit_bytes=VMEM_LIMIT),
        name="hgrn_fwd",
    )(q, kf, kb, lff, lfb, v, g, hist, og)


def _outproj_kernel(x_ref, a_ref, s_ref, wout_ref, gffn_ref, h_ref, hn_ref):
    mix = jnp.concatenate([a_ref[h] for h in range(HEADS)] + [s_ref[...]], axis=1)
    h = x_ref[...] + _dot(mix, wout_ref[...])
    h_ref[...] = h
    hn_ref[...] = _rms(h, gffn_ref[...]).astype(hn_ref.dtype)


def _outproj_call(x2, a, s, wout, gffn, *, seq, tm):
    tokens, d_model = x2.shape
    nlb = seq // tm
    tile = pl.BlockSpec((tm, d_model), lambda i: (i, 0))
    return pl.pallas_call(
        _outproj_kernel,
        grid=(tokens // tm,),
        in_specs=[tile,
                  pl.BlockSpec((None, HEADS, tm, HEAD_DIM), lambda i: (i // nlb, 0, i % nlb, 0)),
                  pl.BlockSpec((tm, s.shape[1]), lambda i: (i, 0)),
                  _resident(wout.shape), _resident(gffn.shape)],
        out_specs=[tile, tile],
        out_shape=[jax.ShapeDtypeStruct((tokens, d_model), F32),
                   jax.ShapeDtypeStruct((tokens, d_model), BF16)],
        compiler_params=pltpu.CompilerParams(
            dimension_semantics=("arbitrary",), vmem_limit_bytes=VMEM_LIMIT),
        name="outproj",
    )(x2, a, s, wout, gffn)


def _ffn_kernel(h_ref, hn_ref, p_ref, wg_ref, wu_ref, wd_ref, gple_ref, wpg_ref, wpp_ref,
                gout_ref, o_ref):
    j = pl.program_id(1)
    hn = hn_ref[...]
    gate = _dot(hn, wg_ref[...])
    act = (gate * _sigmoid(gate) * _dot(hn, wu_ref[...])).astype(BF16)
    part = _dot(act, wd_ref[...])

    @pl.when(j == 0)
    def _():
        o_ref[...] = h_ref[...] + part

    @pl.when(j > 0)
    def _():
        o_ref[...] += part

    @pl.when(j == pl.num_programs(1) - 1)
    def _():
        h2 = o_ref[...]
        hp = _rms(h2, gple_ref[...]).astype(BF16)
        pgate = _sigmoid(_dot(hp, wpg_ref[...]))
        pproj = _dot(p_ref[...].astype(BF16), wpp_ref[...])
        o_ref[...] = _rms(h2 + pgate * pproj, gout_ref[...])


def _ffn_call(h, hn, p2, wg, wu, wd, gple, wpg, wpp, gout, *, tm, th):
    tokens, d_model = h.shape
    hidden = wg.shape[1]
    tile = pl.BlockSpec((tm, d_model), lambda i, j: (i, 0))
    return pl.pallas_call(
        _ffn_kernel,
        grid=(tokens // tm, hidden // th),
        in_specs=[tile, tile,
                  pl.BlockSpec((tm, p2.shape[1]), lambda i, j: (i, 0)),
                  pl.BlockSpec((d_model, th), lambda i, j: (0, j)),
                  pl.BlockSpec((d_model, th), lambda i, j: (0, j)),
                  pl.BlockSpec((th, d_model), lambda i, j: (j, 0)),
                  _resident(gple.shape), _resident(wpg.shape), _resident(wpp.shape),
                  _resident(gout.shape)],
        out_specs=tile,
        out_shape=jax.ShapeDtypeStruct((tokens, d_model), F32),
        compiler_params=pltpu.CompilerParams(
            dimension_semantics=("arbitrary", "arbitrary"), vmem_limit_bytes=VMEM_LIMIT),
        name="ffn_ple",
    )(h, hn, p2, wg, wu, wd, gple, wpg, wpp, gout)


def _layer(h2, p2, lb_f, lb_b, norm_mix_g, w_in, hgrn_onorm_g, sgu_ln_g, sgu_ln_b, sgu_w,
           sgu_b, sgu_onorm_g, w_out, norm_ffn_g, w_gate, w_up, w_down, norm_ple_g,
           w_ple_gate, w_ple_proj, out_g, *, batch, seq):
    row = lambda a: a.reshape(1, -1).astype(F32)
    q, kf, kb, lff, lfb, v, g, s = _inproj_call(
        h2, row(norm_mix_g), w_in.astype(BF16), row(lb_f), row(lb_b), row(sgu_ln_g),
        row(sgu_ln_b), sgu_w.astype(BF16), sgu_b.astype(F32)[:, :, None], row(sgu_onorm_g),
        batch=batch, seq=seq, tm=256)
    flat = lambda t: t.reshape(batch * HEADS, seq, HEAD_DIM)
    hist = _bwd_scan_call(flat(kb), flat(lfb), flat(v), tb=512)
    a = _fwd_call(flat(q), flat(kf), flat(kb), flat(lff), flat(lfb), flat(v), flat(g), hist,
                  row(hgrn_onorm_g), tb=512)
    a = a.reshape(batch, HEADS, seq, HEAD_DIM)
    hmid, hn = _outproj_call(h2, a, s, w_out.astype(BF16), row(norm_ffn_g), seq=seq, tm=512)
    return _ffn_call(hmid, hn, p2, w_gate.astype(BF16), w_up.astype(BF16), w_down.astype(BF16),
                     row(norm_ple_g), w_ple_gate.astype(BF16), w_ple_proj.astype(BF16),
                     row(out_g), tm=512, th=512)


def kernel(x, p, norm_mix_g, w_in, lb_fwd_logits, lb_bwd_logits, hgrn_onorm_g, sgu_ln_g, sgu_ln_b, sgu_w, sgu_b, sgu_onorm_g, w_out, norm_ffn_g, w_gate, w_up, w_down, norm_ple_g, w_ple_gate, w_ple_proj, final_norm_g):
    batch, seq, d_model = x.shape
    depth = w_in.shape[0]
    lb_f_all = jnp.cumsum(jax.nn.softmax(lb_fwd_logits.astype(F32), axis=0), axis=0)
    lb_b_all = jnp.cumsum(jax.nn.softmax(lb_bwd_logits.astype(F32), axis=0), axis=0)
    h = x.reshape(batch * seq, d_model)
    for layer in range(depth):
        assert layer == depth - 1
        h = _layer(h, p[layer].reshape(batch * seq, -1), lb_f_all[layer], lb_b_all[layer],
                   norm_mix_g[layer], w_in[layer], hgrn_onorm_g[layer], sgu_ln_g[layer],
                   sgu_ln_b[layer], sgu_w[layer], sgu_b[layer], sgu_onorm_g[layer],
                   w_out[layer], norm_ffn_g[layer], w_gate[layer], w_up[layer],
                   w_down[layer], norm_ple_g[layer], w_ple_gate[layer], w_ple_proj[layer],
                   final_norm_g, batch=batch, seq=seq)
    return h.reshape(batch, seq, d_model)
```

```python
import jax
import jax.numpy as jnp
from jax import lax
from jax.experimental import pallas as pl
from jax.experimental.pallas import tpu as pltpu

F32 = jnp.float32
BF16 = jnp.bfloat16
EPS = 1e-6
LOG2_E = 1.4426950408889634

HEADS = 8
HEAD_DIM = 128
SUBLANES = 8
SCAN_CHUNK = 64
SGU_CHUNK = 128
LEVELS = (1, 2, 4, 8, 16, 32)
VMEM_LIMIT = 60 * 1024 * 1024


def _dot(a, b):
    return jnp.dot(a, b, preferred_element_type=F32)


def _dot_nt(a, b):
    return lax.dot_general(a, b, (((1,), (1,)), ((), ())), preferred_element_type=F32)


def _dot_tn(a, b):
    return lax.dot_general(a, b, (((0,), (0,)), ((), ())), preferred_element_type=F32)


def _sigmoid(x):
    return 1.0 / (1.0 + jnp.exp(-x))


def _rms(x, g):
    return x * lax.rsqrt(jnp.mean(x * x, axis=-1, keepdims=True) + EPS) * g


def _resident(shape):
    nd = len(shape)
    return pl.BlockSpec(shape, lambda *_: (0,) * nd, pipeline_mode=pl.Buffered(1))


def _chunk_cumsum(x, reverse):
    rows, cols = x.shape
    x3 = x.reshape(rows // SUBLANES, SUBLANES, cols)
    sub = lax.broadcasted_iota(jnp.int32, x3.shape, 1)
    s = 1
    while s < SUBLANES:
        if reverse:
            x3 = x3 + jnp.where(sub < SUBLANES - s, pltpu.roll(x3, SUBLANES - s, axis=1), 0.0)
        else:
            x3 = x3 + jnp.where(sub >= s, pltpu.roll(x3, s, axis=1), 0.0)
        s *= 2
    groups = SCAN_CHUNK // SUBLANES
    edge = 0 if reverse else SUBLANES - 1
    out = [None] * (rows // SUBLANES)
    for c in range(rows // SCAN_CHUNK):
        carry = None
        order = range(groups - 1, -1, -1) if reverse else range(groups)
        for j in order:
            blk = x3[c * groups + j]
            total = blk[edge:edge + 1, :]
            out[c * groups + j] = blk if carry is None else blk + carry
            carry = total if carry is None else carry + total
    return jnp.concatenate(out, axis=0)


def _inproj_kernel(x_ref, gmix_ref, win_ref, lbf_ref, lbb_ref, lng_ref, lnb_ref,
                   sw_ref, sb_ref, sog_ref,
                   q_ref, kf_ref, kb_ref, pf_ref, cb_ref, v_ref, g_ref, s_ref):
    width = HEADS * HEAD_DIM
    tm = x_ref.shape[0]
    xb = _rms(x_ref[...], gmix_ref[...]).astype(BF16)

    def proj(j):
        return _dot(xb, win_ref[:, j * width:(j + 1) * width])

    def put_heads(ref, val):
        for h in range(HEADS):
            ref[h] = val[:, h * HEAD_DIM:(h + 1) * HEAD_DIM].astype(ref.dtype)

    q = proj(0)
    put_heads(q_ref, q * _sigmoid(q))

    def gates(z, lb, k_ref, cum_ref, reverse):
        sig = _sigmoid(z)
        log2_f = jnp.log(lb + (1.0 - lb) * sig) * LOG2_E
        put_heads(cum_ref, _chunk_cumsum(log2_f, reverse))
        put_heads(k_ref, (1.0 - lb) * (1.0 - sig))

    gates(proj(1), lbf_ref[...], kf_ref, pf_ref, False)
    gates(proj(2), lbb_ref[...], kb_ref, cb_ref, True)
    put_heads(v_ref, proj(3))
    g = proj(4)
    put_heads(g_ref, g * _sigmoid(g))

    def gelu(t):
        return 0.5 * t * (1.0 + lax.erf(t * (2.0 ** -0.5)))

    u = gelu(proj(5))
    v = gelu(proj(6))
    mu = jnp.mean(v, axis=-1, keepdims=True)
    vc = v - mu
    v = vc * lax.rsqrt(jnp.mean(vc * vc, axis=-1, keepdims=True) + EPS)
    v = (v * lng_ref[...] + lnb_ref[...]).astype(BF16)
    for c in range(tm // SGU_CHUNK):
        rows = slice(c * SGU_CHUNK, (c + 1) * SGU_CHUNK)
        mixed = [
            _dot(sw_ref[gi], v[rows, gi * HEAD_DIM:(gi + 1) * HEAD_DIM]) + sb_ref[gi]
            for gi in range(HEADS)
        ]
        s = u[rows] * jnp.concatenate(mixed, axis=1)
        s_ref[rows, :] = _rms(s, sog_ref[...]).astype(s_ref.dtype)


def _inproj_call(x2, gmix, win, lbf, lbb, lng, lnb, sw, sb, sog, *, batch, seq, tm):
    tokens, d_model = x2.shape
    width = HEADS * HEAD_DIM
    nlb = seq // tm
    row = lambda a: _resident(a.shape)
    head_spec = pl.BlockSpec((None, HEADS, tm, HEAD_DIM), lambda i: (i // nlb, 0, i % nlb, 0))
    head_shape = lambda dt: jax.ShapeDtypeStruct((batch, HEADS, seq, HEAD_DIM), dt)
    return pl.pallas_call(
        _inproj_kernel,
        grid=(tokens // tm,),
        in_specs=[pl.BlockSpec((tm, d_model), lambda i: (i, 0)),
                  row(gmix), row(win), row(lbf), row(lbb), row(lng), row(lnb),
                  row(sw), row(sb), row(sog)],
        out_specs=[head_spec] * 7 + [pl.BlockSpec((tm, width), lambda i: (i, 0))],
        out_shape=[head_shape(BF16), head_shape(BF16), head_shape(BF16),
                   head_shape(F32), head_shape(F32), head_shape(BF16), head_shape(BF16),
                   jax.ShapeDtypeStruct((tokens, width), BF16)],
        compiler_params=pltpu.CompilerParams(
            dimension_semantics=("arbitrary",), vmem_limit_bytes=VMEM_LIMIT),
        name="inproj",
    )(x2, gmix, win, lbf, lbb, lng, lnb, sw, sb, sog)


def _chunk_masks():
    sub = lax.broadcasted_iota(jnp.int32, (SCAN_CHUNK, HEAD_DIM), 0) & (SUBLANES - 1)
    pair_xor = (lax.broadcasted_iota(jnp.int32, (SCAN_CHUNK, SCAN_CHUNK), 0)
                ^ lax.broadcasted_iota(jnp.int32, (SCAN_CHUNK, SCAN_CHUNK), 1))
    late = {h: (sub & h) != 0 for h in LEVELS if h < SUBLANES}
    return late, sub < SUBLANES // 2, {h: pair_xor >= h for h in LEVELS}


def _sublane_ref(x, rows_per_group, r0, low_half):
    x3 = x.reshape(SCAN_CHUNK // SUBLANES, SUBLANES, HEAD_DIM)
    pick = lambda r: jnp.broadcast_to(x3[:, r:r + 1, :], x3.shape).reshape(x.shape)
    if rows_per_group == SUBLANES:
        return pick(r0)
    return jnp.where(low_half, pick(r0), pick(r0 + rows_per_group))


def _level_operands(q, kf, kb, pf, cb, h, late_rows, low_half):
    if h >= SUBLANES:
        lhs, rhs = [], []
        for r0 in range(0, SCAN_CHUNK, 2 * h):
            early, late = slice(r0, r0 + h), slice(r0 + h, r0 + 2 * h)
            ref_f = pf[r0 + h - 1:r0 + h, :]
            ref_b = cb[r0 + h:r0 + h + 1, :]
            lhs += [q[early] * jnp.exp2(cb[early] - ref_b), q[late] * jnp.exp2(pf[late] - ref_f)]
            rhs += [kf[early] * jnp.exp2(ref_f - pf[early]), kb[late] * jnp.exp2(ref_b - cb[late])]
        return jnp.concatenate(lhs, axis=0), jnp.concatenate(rhs, axis=0)
    late = late_rows[h]
    k_sel = jnp.where(late, kb, kf)
    if h == 1:
        d_f = pf - pltpu.roll(pf, 1, axis=0)
        d_b = cb - pltpu.roll(cb, SCAN_CHUNK - 1, axis=0)
        return q * jnp.exp2(jnp.where(late, d_f, d_b)), k_sel
    d_f = pf - _sublane_ref(pf, 2 * h, h - 1, low_half)
    d_b = cb - _sublane_ref(cb, 2 * h, h, low_half)
    lhs = q * jnp.exp2(jnp.where(late, d_f, d_b))
    rhs = k_sel * jnp.exp2(-jnp.where(late, d_b, d_f))
    return lhs, rhs


def _intra_scores(q, kf, kb, pf, cb, masks):
    late_rows, low_half, pair_level = masks
    a = jnp.zeros((SCAN_CHUNK, SCAN_CHUNK), F32)
    for h in LEVELS:
        lhs, rhs = _level_operands(q, kf, kb, pf, cb, h, late_rows, low_half)
        g = _dot_nt(lhs.astype(BF16), rhs.astype(BF16))
        a = jnp.where(pair_level[h], g, a)
    return a


def _bwd_scan_kernel(kb_ref, cb_ref, v_ref, hist_ref, st_ref):
    heads = kb_ref.shape[0]
    nchunk = kb_ref.shape[1] // SCAN_CHUNK

    @pl.when(pl.program_id(1) == 0)
    def _():
        st_ref[...] = jnp.zeros_like(st_ref)

    def body(ci, carry):
        c = nchunk - 1 - ci
        rows = pl.ds(pl.multiple_of(c * SCAN_CHUNK, SCAN_CHUNK), SCAN_CHUNK)
        for h in range(heads):
            cb = cb_ref[h, rows, :]
            st = st_ref[h]
            hist_ref[h, c] = st.astype(hist_ref.dtype)
            tot = cb[0:1, :]
            kt = (kb_ref[h, rows, :].astype(F32) * jnp.exp2(tot - cb)).astype(BF16)
            st_ref[h] = st * jnp.exp2(tot) + _dot_tn(v_ref[h, rows, :], kt)
        return carry

    lax.fori_loop(0, nchunk, body, 0)


def _bwd_scan_call(kb, cb, v, *, tb):
    batch, heads, seq, _ = kb.shape
    nblk = seq // tb
    spec = pl.BlockSpec((None, heads, tb, HEAD_DIM), lambda b, i: (b, 0, nblk - 1 - i, 0))
    return pl.pallas_call(
        _bwd_scan_kernel,
        grid=(batch, nblk),
        in_specs=[spec, spec, spec],
        out_specs=pl.BlockSpec((None, heads, tb // SCAN_CHUNK, HEAD_DIM, HEAD_DIM),
                               lambda b, i: (b, 0, nblk - 1 - i, 0, 0)),
        out_shape=jax.ShapeDtypeStruct(
            (batch, heads, seq // SCAN_CHUNK, HEAD_DIM, HEAD_DIM), BF16),
        scratch_shapes=[pltpu.VMEM((heads, HEAD_DIM, HEAD_DIM), F32)],
        compiler_params=pltpu.CompilerParams(
            dimension_semantics=("arbitrary", "arbitrary"), vmem_limit_bytes=VMEM_LIMIT),
        name="hgrn_bwd_scan",
    )(kb, cb, v)


def _fwd_kernel(q_ref, kf_ref, kb_ref, pf_ref, cb_ref, v_ref, g_ref, hist_ref, og_ref,
                a_ref, st_ref):
    heads = q_ref.shape[0]
    nchunk = q_ref.shape[1] // SCAN_CHUNK

    @pl.when(pl.program_id(2) == 0)
    def _():
        st_ref[...] = jnp.zeros_like(st_ref)

    def head_chunk(h, c, rows, masks):
        q = q_ref[h, rows, :].astype(F32)
        kf = kf_ref[h, rows, :].astype(F32)
        kb = kb_ref[h, rows, :].astype(F32)
        pf = pf_ref[h, rows, :]
        cb = cb_ref[h, rows, :]
        v = v_ref[h, rows, :]

        a = _intra_scores(q, kf, kb, pf, cb, masks)
        diag = jnp.sum(q * (kf + kb), axis=-1, keepdims=True)
        o = _dot(a.astype(BF16), v) + diag * v.astype(F32)

        st = st_ref[h]
        q_in = jnp.concatenate([q * jnp.exp2(pf), q * jnp.exp2(cb)], axis=1)
        s_in = jnp.concatenate([st.astype(BF16), hist_ref[h, c]], axis=1)
        o = o + _dot_nt(q_in.astype(BF16), s_in)

        tot = pf[SCAN_CHUNK - 1:SCAN_CHUNK, :]
        kt = (kf * jnp.exp2(tot - pf)).astype(BF16)
        st_ref[h] = st * jnp.exp2(tot) + _dot_tn(v, kt)

        o = _rms(o, og_ref[...]) * g_ref[h, rows, :].astype(F32)
        a_ref[h, rows, :] = o.astype(a_ref.dtype)

    def body(c, carry):
        rows = pl.ds(pl.multiple_of(c * SCAN_CHUNK, SCAN_CHUNK), SCAN_CHUNK)
        masks = _chunk_masks()
        for h in range(heads):
            head_chunk(h, c, rows, masks)
        return carry

    lax.fori_loop(0, nchunk, body, 0)


def _fwd_call(q, kf, kb, pf, cb, v, g, hist, og, *, tb, hp):
    batch, heads, seq, _ = q.shape
    spec = pl.BlockSpec((None, hp, tb, HEAD_DIM), lambda b, hg, i: (b, hg, i, 0))
    return pl.pallas_call(
        _fwd_kernel,
        grid=(batch, heads // hp, seq // tb),
        in_specs=[spec] * 7 + [
            pl.BlockSpec((None, hp, tb // SCAN_CHUNK, HEAD_DIM, HEAD_DIM),
                         lambda b, hg, i: (b, hg, i, 0, 0)),
            _resident(og.shape)],
        out_specs=spec,
        out_shape=jax.ShapeDtypeStruct((batch, heads, seq, HEAD_DIM), BF16),
        scratch_shapes=[pltpu.VMEM((hp, HEAD_DIM, HEAD_DIM), F32)],
        compiler_params=pltpu.CompilerParams(
            dimension_semantics=("arbitrary", "arbitrary", "arbitrary"),
            vmem_limit_bytes=VMEM_LIMIT),
        name="hgrn_fwd",
    )(q, kf, kb, pf, cb, v, g, hist, og)


def _outproj_kernel(x_ref, a_ref, s_ref, wout_ref, gffn_ref, h_ref, hn_ref):
    mix = jnp.concatenate([a_ref[h] for h in range(HEADS)] + [s_ref[...]], axis=1)
    h = x_ref[...] + _dot(mix, wout_ref[...])
    h_ref[...] = h
    hn_ref[...] = _rms(h, gffn_ref[...]).astype(hn_ref.dtype)


def _outproj_call(x2, a, s, wout, gffn, *, seq, tm):
    tokens, d_model = x2.shape
    nlb = seq // tm
    tile = pl.BlockSpec((tm, d_model), lambda i: (i, 0))
    return pl.pallas_call(
        _outproj_kernel,
        grid=(tokens // tm,),
        in_specs=[tile,
                  pl.BlockSpec((None, HEADS, tm, HEAD_DIM), lambda i: (i // nlb, 0, i % nlb, 0)),
                  pl.BlockSpec((tm, s.shape[1]), lambda i: (i, 0)),
                  _resident(wout.shape), _resident(gffn.shape)],
        out_specs=[tile, tile],
        out_shape=[jax.ShapeDtypeStruct((tokens, d_model), F32),
                   jax.ShapeDtypeStruct((tokens, d_model), BF16)],
        compiler_params=pltpu.CompilerParams(
            dimension_semantics=("arbitrary",), vmem_limit_bytes=VMEM_LIMIT),
        name="outproj",
    )(x2, a, s, wout, gffn)


def _ffn_kernel(h_ref, hn_ref, p_ref, wg_ref, wu_ref, wd_ref, gple_ref, wpg_ref, wpp_ref,
                gout_ref, o_ref):
    j = pl.program_id(1)

    @pl.when(j == 0)
    def _():
        o_ref[...] = h_ref[...]

    hn = hn_ref[...]
    gate = _dot(hn, wg_ref[...])
    act = (gate * _sigmoid(gate) * _dot(hn, wu_ref[...])).astype(BF16)
    o_ref[...] += _dot(act, wd_ref[...])

    @pl.when(j == pl.num_programs(1) - 1)
    def _():
        h2 = o_ref[...]
        hp = _rms(h2, gple_ref[...]).astype(BF16)
        pgate = _sigmoid(_dot(hp, wpg_ref[...]))
        pproj = _dot(p_ref[...].astype(BF16), wpp_ref[...])
        o_ref[...] = _rms(h2 + pgate * pproj, gout_ref[...])


def _ffn_call(h, hn, p2, wg, wu, wd, gple, wpg, wpp, gout, *, tm, th):
    tokens, d_model = h.shape
    hidden = wg.shape[1]
    tile = pl.BlockSpec((tm, d_model), lambda i, j: (i, 0))
    return pl.pallas_call(
        _ffn_kernel,
        grid=(tokens // tm, hidden // th),
        in_specs=[tile, tile,
                  pl.BlockSpec((tm, p2.shape[1]), lambda i, j: (i, 0)),
                  pl.BlockSpec((d_model, th), lambda i, j: (0, j)),
                  pl.BlockSpec((d_model, th), lambda i, j: (0, j)),
                  pl.BlockSpec((th, d_model), lambda i, j: (j, 0)),
                  _resident(gple.shape), _resident(wpg.shape), _resident(wpp.shape),
                  _resident(gout.shape)],
        out_specs=tile,
        out_shape=jax.ShapeDtypeStruct((tokens, d_model), F32),
        compiler_params=pltpu.CompilerParams(
            dimension_semantics=("arbitrary", "arbitrary"), vmem_limit_bytes=VMEM_LIMIT),
        name="ffn_ple",
    )(h, hn, p2, wg, wu, wd, gple, wpg, wpp, gout)


def _layer(h2, p2, lb_f, lb_b, norm_mix_g, w_in, hgrn_onorm_g, sgu_ln_g, sgu_ln_b, sgu_w,
           sgu_b, sgu_onorm_g, w_out, norm_ffn_g, w_gate, w_up, w_down, norm_ple_g,
           w_ple_gate, w_ple_proj, out_g, *, batch, seq):
    row = lambda a: a.reshape(1, -1).astype(F32)
    q, kf, kb, pf, cb, v, g, s = _inproj_call(
        h2, row(norm_mix_g), w_in.astype(BF16), row(lb_f), row(lb_b), row(sgu_ln_g),
        row(sgu_ln_b), sgu_w.astype(BF16), sgu_b.astype(F32)[:, :, None], row(sgu_onorm_g),
        batch=batch, seq=seq, tm=256)
    hist = _bwd_scan_call(kb, cb, v, tb=512)
    a = _fwd_call(q, kf, kb, pf, cb, v, g, hist, row(hgrn_onorm_g), tb=512, hp=8)
    hmid, hn = _outproj_call(h2, a, s, w_out.astype(BF16), row(norm_ffn_g), seq=seq, tm=512)
    return _ffn_call(hmid, hn, p2, w_gate.astype(BF16), w_up.astype(BF16), w_down.astype(BF16),
                     row(norm_ple_g), w_ple_gate.astype(BF16), w_ple_proj.astype(BF16),
                     row(out_g), tm=512, th=512)


def kernel(x, p, norm_mix_g, w_in, lb_fwd_logits, lb_bwd_logits, hgrn_onorm_g, sgu_ln_g, sgu_ln_b, sgu_w, sgu_b, sgu_onorm_g, w_out, norm_ffn_g, w_gate, w_up, w_down, norm_ple_g, w_ple_gate, w_ple_proj, final_norm_g):
    batch, seq, d_model = x.shape
    depth = w_in.shape[0]
    lb_f_all = jnp.cumsum(jax.nn.softmax(lb_fwd_logits.astype(F32), axis=0), axis=0)
    lb_b_all = jnp.cumsum(jax.nn.softmax(lb_bwd_logits.astype(F32), axis=0), axis=0)
    h = x.reshape(batch * seq, d_model)
    for layer in range(depth):
        assert layer == depth - 1
        h = _layer(h, p[layer].reshape(batch * seq, -1), lb_f_all[layer], lb_b_all[layer],
                   norm_mix_g[layer], w_in[layer], hgrn_onorm_g[layer], sgu_ln_g[layer],
                   sgu_ln_b[layer], sgu_w[layer], sgu_b[layer], sgu_onorm_g[layer],
                   w_out[layer], norm_ffn_g[layer], w_gate[layer], w_up[layer],
                   w_down[layer], norm_ple_g[layer], w_ple_gate[layer], w_ple_proj[layer],
                   final_norm_g, batch=batch, seq=seq)
    return h.reshape(batch, seq, d_model)
```

```python
import jax
import jax.numpy as jnp
from jax import lax
from jax.experimental import pallas as pl
from jax.experimental.pallas import tpu as pltpu

F32 = jnp.float32
BF16 = jnp.bfloat16
EPS = 1e-6

HEADS = 8
HEAD_DIM = 128
SUBLANES = 8
SCAN_CHUNK = 64
SGU_CHUNK = 128
LEVELS = (1, 2, 4, 8, 16, 32)
VMEM_LIMIT = 60 * 1024 * 1024


def _dot(a, b):
    return jnp.dot(a, b, preferred_element_type=F32)


def _dot_nt(a, b):
    return lax.dot_general(a, b, (((1,), (1,)), ((), ())), preferred_element_type=F32)


def _dot_tn(a, b):
    return lax.dot_general(a, b, (((0,), (0,)), ((), ())), preferred_element_type=F32)


def _sigmoid(x):
    return 1.0 / (1.0 + jnp.exp(-x))


def _rms(x, g):
    return x * lax.rsqrt(jnp.mean(x * x, axis=-1, keepdims=True) + EPS) * g


def _resident(shape):
    nd = len(shape)
    return pl.BlockSpec(shape, lambda *_: (0,) * nd, pipeline_mode=pl.Buffered(1))


def _chunk_cumsum(x, reverse):
    rows, cols = x.shape
    x3 = x.reshape(rows // SUBLANES, SUBLANES, cols)
    sub = lax.broadcasted_iota(jnp.int32, x3.shape, 1)
    s = 1
    while s < SUBLANES:
        if reverse:
            x3 = x3 + jnp.where(sub < SUBLANES - s, pltpu.roll(x3, SUBLANES - s, axis=1), 0.0)
        else:
            x3 = x3 + jnp.where(sub >= s, pltpu.roll(x3, s, axis=1), 0.0)
        s *= 2
    groups = SCAN_CHUNK // SUBLANES
    edge = 0 if reverse else SUBLANES - 1
    out = [None] * (rows // SUBLANES)
    for c in range(rows // SCAN_CHUNK):
        carry = None
        order = range(groups - 1, -1, -1) if reverse else range(groups)
        for j in order:
            blk = x3[c * groups + j]
            total = blk[edge:edge + 1, :]
            out[c * groups + j] = blk if carry is None else blk + carry
            carry = total if carry is None else carry + total
    return jnp.concatenate(out, axis=0)


def _sgu(u_raw, v_raw, lng, lnb, sw_ref, sb_ref, sog, s_ref):
    def gelu(t):
        return 0.5 * t * (1.0 + lax.erf(t * (2.0 ** -0.5)))

    v = gelu(v_raw)
    vc = v - jnp.mean(v, axis=-1, keepdims=True)
    v = vc * lax.rsqrt(jnp.mean(vc * vc, axis=-1, keepdims=True) + EPS)
    v = (v * lng + lnb).astype(BF16)
    u = gelu(u_raw)
    for c in range(u.shape[0] // SGU_CHUNK):
        rows = slice(c * SGU_CHUNK, (c + 1) * SGU_CHUNK)
        mixed = [
            _dot(sw_ref[gi], v[rows, gi * HEAD_DIM:(gi + 1) * HEAD_DIM]) + sb_ref[gi]
            for gi in range(HEADS)
        ]
        s_ref[rows, :] = _rms(u[rows] * jnp.concatenate(mixed, axis=1), sog).astype(s_ref.dtype)


def _inproj_kernel(x_ref, gmix_ref, win_ref, lbf_ref, lbb_ref, lng_ref, lnb_ref,
                   sw_ref, sb_ref, sog_ref,
                   q_ref, kf_ref, kb_ref, pf_ref, cb_ref, v_ref, g_ref, s_ref):
    width = HEADS * HEAD_DIM
    xb = _rms(x_ref[...], gmix_ref[...]).astype(BF16)

    def proj(j):
        return _dot(xb, win_ref[:, j * width:(j + 1) * width])

    def put_heads(ref, val):
        for h in range(HEADS):
            ref[h] = val[:, h * HEAD_DIM:(h + 1) * HEAD_DIM].astype(ref.dtype)

    v_raw = proj(6)
    _sgu(proj(5), v_raw, lng_ref[...], lnb_ref[...], sw_ref, sb_ref, sog_ref[...], s_ref)

    def gates(j, lb_ref, k_ref, cum_ref, reverse):
        half = width // 2
        for part in range(2):
            cols = slice(part * half, (part + 1) * half)
            lb = lb_ref[:, cols]
            z = _dot(xb, win_ref[:, j * width + part * half:j * width + (part + 1) * half])
            sig = _sigmoid(z)
            cum = _chunk_cumsum(jnp.log2(lb + (1.0 - lb) * sig), reverse)
            k = (1.0 - lb) * (1.0 - sig)
            for h in range(HEADS // 2):
                hh = part * (HEADS // 2) + h
                cum_ref[hh] = cum[:, h * HEAD_DIM:(h + 1) * HEAD_DIM]
                k_ref[hh] = k[:, h * HEAD_DIM:(h + 1) * HEAD_DIM].astype(k_ref.dtype)

    gates(1, lbf_ref, kf_ref, pf_ref, False)
    q = proj(0)
    put_heads(q_ref, q * _sigmoid(q))
    gates(2, lbb_ref, kb_ref, cb_ref, True)
    g = proj(4)
    put_heads(g_ref, g * _sigmoid(g))
    put_heads(v_ref, proj(3))


def _inproj_call(x2, gmix, win, lbf, lbb, lng, lnb, sw, sb, sog, *, batch, seq, tm):
    tokens, d_model = x2.shape
    width = HEADS * HEAD_DIM
    nlb = seq // tm
    row = lambda a: _resident(a.shape)
    head_spec = pl.BlockSpec((None, HEADS, tm, HEAD_DIM), lambda i: (i // nlb, 0, i % nlb, 0))
    head_shape = lambda dt: jax.ShapeDtypeStruct((batch, HEADS, seq, HEAD_DIM), dt)
    return pl.pallas_call(
        _inproj_kernel,
        grid=(tokens // tm,),
        in_specs=[pl.BlockSpec((tm, d_model), lambda i: (i, 0)),
                  row(gmix), row(win), row(lbf), row(lbb), row(lng), row(lnb),
                  row(sw), row(sb), row(sog)],
        out_specs=[head_spec] * 7 + [pl.BlockSpec((tm, width), lambda i: (i, 0))],
        out_shape=[head_shape(BF16), head_shape(BF16), head_shape(BF16),
                   head_shape(F32), head_shape(F32), head_shape(BF16), head_shape(BF16),
                   jax.ShapeDtypeStruct((tokens, width), BF16)],
        compiler_params=pltpu.CompilerParams(
            dimension_semantics=("arbitrary",), vmem_limit_bytes=VMEM_LIMIT),
        name="inproj",
    )(x2, gmix, win, lbf, lbb, lng, lnb, sw, sb, sog)


def _chunk_masks():
    sub = lax.broadcasted_iota(jnp.int32, (SCAN_CHUNK, HEAD_DIM), 0) & (SUBLANES - 1)
    pair_xor = (lax.broadcasted_iota(jnp.int32, (SCAN_CHUNK, SCAN_CHUNK), 0)
                ^ lax.broadcasted_iota(jnp.int32, (SCAN_CHUNK, SCAN_CHUNK), 1))
    late = {h: (sub & h) != 0 for h in LEVELS if h < SUBLANES}
    return late, sub < SUBLANES // 2, {h: pair_xor >= h for h in LEVELS}


def _sublane_ref(x, rows_per_group, r0, low_half):
    x3 = x.reshape(SCAN_CHUNK // SUBLANES, SUBLANES, HEAD_DIM)
    pick = lambda r: jnp.broadcast_to(x3[:, r:r + 1, :], x3.shape).reshape(x.shape)
    if rows_per_group == SUBLANES:
        return pick(r0)
    return jnp.where(low_half, pick(r0), pick(r0 + rows_per_group))


def _level_operands(q, kf, kb, pf, cb, h, late_rows, low_half):
    if h >= SUBLANES:
        lhs, rhs = [], []
        for r0 in range(0, SCAN_CHUNK, 2 * h):
            early, late = slice(r0, r0 + h), slice(r0 + h, r0 + 2 * h)
            ref_f = pf[r0 + h - 1:r0 + h, :]
            ref_b = cb[r0 + h:r0 + h + 1, :]
            lhs += [q[early] * jnp.exp2(cb[early] - ref_b), q[late] * jnp.exp2(pf[late] - ref_f)]
            rhs += [kf[early] * jnp.exp2(ref_f - pf[early]), kb[late] * jnp.exp2(ref_b - cb[late])]
        return jnp.concatenate(lhs, axis=0), jnp.concatenate(rhs, axis=0)
    late = late_rows[h]
    k_sel = jnp.where(late, kb, kf)
    if h == 1:
        d_f = pf - pltpu.roll(pf, 1, axis=0)
        d_b = cb - pltpu.roll(cb, SCAN_CHUNK - 1, axis=0)
        return q * jnp.exp2(jnp.where(late, d_f, d_b)), k_sel
    d_f = pf - _sublane_ref(pf, 2 * h, h - 1, low_half)
    d_b = cb - _sublane_ref(cb, 2 * h, h, low_half)
    lhs = q * jnp.exp2(jnp.where(late, d_f, d_b))
    rhs = k_sel * jnp.exp2(-jnp.where(late, d_b, d_f))
    return lhs, rhs


def _intra_scores(q, kf, kb, pf, cb, masks):
    late_rows, low_half, pair_level = masks
    a = jnp.zeros((SCAN_CHUNK, SCAN_CHUNK), F32)
    for h in LEVELS:
        lhs, rhs = _level_operands(q, kf, kb, pf, cb, h, late_rows, low_half)
        g = _dot_nt(lhs.astype(BF16), rhs.astype(BF16))
        a = jnp.where(pair_level[h], g, a)
    return a


def _bwd_scan_kernel(kb_ref, cb_ref, v_ref, hist_ref, st_ref):
    heads = kb_ref.shape[0]
    nchunk = kb_ref.shape[1] // SCAN_CHUNK

    @pl.when(pl.program_id(1) == 0)
    def _():
        st_ref[...] = jnp.zeros_like(st_ref)

    def body(ci, carry):
        c = nchunk - 1 - ci
        rows = pl.ds(pl.multiple_of(c * SCAN_CHUNK, SCAN_CHUNK), SCAN_CHUNK)
        for h in range(heads):
            cb = cb_ref[h, rows, :]
            st = st_ref[h]
            hist_ref[h, c] = st.astype(hist_ref.dtype)
            tot = cb[0:1, :]
            kt = (kb_ref[h, rows, :].astype(F32) * jnp.exp2(tot - cb)).astype(BF16)
            st_ref[h] = st * jnp.exp2(tot) + _dot_tn(v_ref[h, rows, :], kt)
        return carry

    lax.fori_loop(0, nchunk, body, 0, unroll=8)


def _bwd_scan_call(kb, cb, v, *, tb):
    batch, heads, seq, _ = kb.shape
    nblk = seq // tb
    spec = pl.BlockSpec((None, heads, tb, HEAD_DIM), lambda b, i: (b, 0, nblk - 1 - i, 0))
    return pl.pallas_call(
        _bwd_scan_kernel,
        grid=(batch, nblk),
        in_specs=[spec, spec, spec],
        out_specs=pl.BlockSpec((None, heads, tb // SCAN_CHUNK, HEAD_DIM, HEAD_DIM),
                               lambda b, i: (b, 0, nblk - 1 - i, 0, 0)),
        out_shape=jax.ShapeDtypeStruct(
            (batch, heads, seq // SCAN_CHUNK, HEAD_DIM, HEAD_DIM), BF16),
        scratch_shapes=[pltpu.VMEM((heads, HEAD_DIM, HEAD_DIM), F32)],
        compiler_params=pltpu.CompilerParams(
            dimension_semantics=("arbitrary", "arbitrary"), vmem_limit_bytes=VMEM_LIMIT),
        name="hgrn_bwd_scan",
    )(kb, cb, v)


def _fwd_kernel(q_ref, kf_ref, kb_ref, pf_ref, cb_ref, v_ref, g_ref, hist_ref, og_ref,
                a_ref, st_ref):
    heads = q_ref.shape[0]
    nchunk = q_ref.shape[1] // SCAN_CHUNK

    @pl.when(pl.program_id(2) == 0)
    def _():
        st_ref[...] = jnp.zeros_like(st_ref)

    def head_chunk(h, c, rows, masks):
        q = q_ref[h, rows, :].astype(F32)
        kf = kf_ref[h, rows, :].astype(F32)
        kb = kb_ref[h, rows, :].astype(F32)
        pf = pf_ref[h, rows, :]
        cb = cb_ref[h, rows, :]
        v = v_ref[h, rows, :]

        a = _intra_scores(q, kf, kb, pf, cb, masks)
        diag = jnp.sum(q * (kf + kb), axis=-1, keepdims=True)
        o = _dot(a.astype(BF16), v) + diag * v.astype(F32)

        st = st_ref[h]
        q_in = jnp.concatenate([q * jnp.exp2(pf), q * jnp.exp2(cb)], axis=1)
        s_in = jnp.concatenate([st.astype(BF16), hist_ref[h, c]], axis=1)
        o = o + _dot_nt(q_in.astype(BF16), s_in)

        tot = pf[SCAN_CHUNK - 1:SCAN_CHUNK, :]
        kt = (kf * jnp.exp2(tot - pf)).astype(BF16)
        st_ref[h] = st * jnp.exp2(tot) + _dot_tn(v, kt)

        o = _rms(o, og_ref[...]) * g_ref[h, rows, :].astype(F32)
        a_ref[h, rows, :] = o.astype(a_ref.dtype)

    def body(c, carry):
        rows = pl.ds(pl.multiple_of(c * SCAN_CHUNK, SCAN_CHUNK), SCAN_CHUNK)
        masks = _chunk_masks()
        for h in range(heads):
            head_chunk(h, c, rows, masks)
        return carry

    lax.fori_loop(0, nchunk, body, 0, unroll=4)


def _fwd_call(q, kf, kb, pf, cb, v, g, hist, og, *, tb, hp):
    batch, heads, seq, _ = q.shape
    spec = pl.BlockSpec((None, hp, tb, HEAD_DIM), lambda b, hg, i: (b, hg, i, 0))
    return pl.pallas_call(
        _fwd_kernel,
        grid=(batch, heads // hp, seq // tb),
        in_specs=[spec] * 7 + [
            pl.BlockSpec((None, hp, tb // SCAN_CHUNK, HEAD_DIM, HEAD_DIM),
                         lambda b, hg, i: (b, hg, i, 0, 0)),
            _resident(og.shape)],
        out_specs=spec,
        out_shape=jax.ShapeDtypeStruct((batch, heads, seq, HEAD_DIM), BF16),
        scratch_shapes=[pltpu.VMEM((hp, HEAD_DIM, HEAD_DIM), F32)],
        compiler_params=pltpu.CompilerParams(
            dimension_semantics=("arbitrary", "arbitrary", "arbitrary"),
            vmem_limit_bytes=VMEM_LIMIT),
        name="hgrn_fwd",
    )(q, kf, kb, pf, cb, v, g, hist, og)


def _outproj_kernel(x_ref, a_ref, s_ref, wout_ref, gffn_ref, h_ref, hn_ref):
    mix = jnp.concatenate([a_ref[h] for h in range(HEADS)] + [s_ref[...]], axis=1)
    h = x_ref[...] + _dot(mix, wout_ref[...])
    h_ref[...] = h
    hn_ref[...] = _rms(h, gffn_ref[...]).astype(hn_ref.dtype)


def _outproj_call(x2, a, s, wout, gffn, *, seq, tm):
    tokens, d_model = x2.shape
    nlb = seq // tm
    tile = pl.BlockSpec((tm, d_model), lambda i: (i, 0))
    return pl.pallas_call(
        _outproj_kernel,
        grid=(tokens // tm,),
        in_specs=[tile,
                  pl.BlockSpec((None, HEADS, tm, HEAD_DIM), lambda i: (i // nlb, 0, i % nlb, 0)),
                  pl.BlockSpec((tm, s.shape[1]), lambda i: (i, 0)),
                  _resident(wout.shape), _resident(gffn.shape)],
        out_specs=[tile, tile],
        out_shape=[jax.ShapeDtypeStruct((tokens, d_model), F32),
                   jax.ShapeDtypeStruct((tokens, d_model), BF16)],
        compiler_params=pltpu.CompilerParams(
            dimension_semantics=("arbitrary",), vmem_limit_bytes=VMEM_LIMIT),
        name="outproj",
    )(x2, a, s, wout, gffn)


def _ffn_kernel(h_ref, hn_ref, p_ref, wg_ref, wu_ref, wd_ref, gple_ref, wpg_ref, wpp_ref,
                gout_ref, o_ref):
    j = pl.program_id(1)

    @pl.when(j == 0)
    def _():
        o_ref[...] = h_ref[...]

    hn = hn_ref[...]
    gate = _dot(hn, wg_ref[...])
    act = (gate * _sigmoid(gate) * _dot(hn, wu_ref[...])).astype(BF16)
    o_ref[...] += _dot(act, wd_ref[...])

    @pl.when(j == pl.num_programs(1) - 1)
    def _():
        h2 = o_ref[...]
        hp = _rms(h2, gple_ref[...]).astype(BF16)
        pgate = _sigmoid(_dot(hp, wpg_ref[...]))
        pproj = _dot(p_ref[...].astype(BF16), wpp_ref[...])
        o_ref[...] = _rms(h2 + pgate * pproj, gout_ref[...])


def _ffn_call(h, hn, p2, wg, wu, wd, gple, wpg, wpp, gout, *, tm, th):
    tokens, d_model = h.shape
    hidden = wg.shape[1]
    tile = pl.BlockSpec((tm, d_model), lambda i, j: (i, 0))
    return pl.pallas_call(
        _ffn_kernel,
        grid=(tokens // tm, hidden // th),
        in_specs=[tile, tile,
                  pl.BlockSpec((tm, p2.shape[1]), lambda i, j: (i, 0)),
                  pl.BlockSpec((d_model, th), lambda i, j: (0, j)),
                  pl.BlockSpec((d_model, th), lambda i, j: (0, j)),
                  pl.BlockSpec((th, d_model), lambda i, j: (j, 0)),
                  _resident(gple.shape), _resident(wpg.shape), _resident(wpp.shape),
                  _resident(gout.shape)],
        out_specs=tile,
        out_shape=jax.ShapeDtypeStruct((tokens, d_model), F32),
        compiler_params=pltpu.CompilerParams(
            dimension_semantics=("arbitrary", "arbitrary"), vmem_limit_bytes=VMEM_LIMIT),
        name="ffn_ple",
    )(h, hn, p2, wg, wu, wd, gple, wpg, wpp, gout)


def _layer(h2, p2, lb_f, lb_b, norm_mix_g, w_in, hgrn_onorm_g, sgu_ln_g, sgu_ln_b, sgu_w,
           sgu_b, sgu_onorm_g, w_out, norm_ffn_g, w_gate, w_up, w_down, norm_ple_g,
           w_ple_gate, w_ple_proj, out_g, *, batch, seq):
    row = lambda a: a.reshape(1, -1).astype(F32)
    q, kf, kb, pf, cb, v, g, s = _inproj_call(
        h2, row(norm_mix_g), w_in.astype(BF16), row(lb_f), row(lb_b), row(sgu_ln_g),
        row(sgu_ln_b), sgu_w.astype(BF16), sgu_b.astype(F32)[:, :, None], row(sgu_onorm_g),
        batch=batch, seq=seq, tm=256)
    hist = _bwd_scan_call(kb, cb, v, tb=512)
    a = _fwd_call(q, kf, kb, pf, cb, v, g, hist, row(hgrn_onorm_g), tb=512, hp=8)
    hmid, hn = _outproj_call(h2, a, s, w_out.astype(BF16), row(norm_ffn_g), seq=seq, tm=512)
    return _ffn_call(hmid, hn, p2, w_gate.astype(BF16), w_up.astype(BF16), w_down.astype(BF16),
                     row(norm_ple_g), w_ple_gate.astype(BF16), w_ple_proj.astype(BF16),
                     row(out_g), tm=512, th=512)


def kernel(x, p, norm_mix_g, w_in, lb_fwd_logits, lb_bwd_logits, hgrn_onorm_g, sgu_ln_g, sgu_ln_b, sgu_w, sgu_b, sgu_onorm_g, w_out, norm_ffn_g, w_gate, w_up, w_down, norm_ple_g, w_ple_gate, w_ple_proj, final_norm_g):
    batch, seq, d_model = x.shape
    depth = w_in.shape[0]
    lb_f_all = jnp.cumsum(jax.nn.softmax(lb_fwd_logits.astype(F32), axis=0), axis=0)
    lb_b_all = jnp.cumsum(jax.nn.softmax(lb_bwd_logits.astype(F32), axis=0), axis=0)
    h = x.reshape(batch * seq, d_model)
    for layer in range(depth):
        assert layer == depth - 1
        h = _layer(h, p[layer].reshape(batch * seq, -1), lb_f_all[layer], lb_b_all[layer],
                   norm_mix_g[layer], w_in[layer], hgrn_onorm_g[layer], sgu_ln_g[layer],
                   sgu_ln_b[layer], sgu_w[layer], sgu_b[layer], sgu_onorm_g[layer],
                   w_out[layer], norm_ffn_g[layer], w_gate[layer], w_up[layer],
                   w_down[layer], norm_ple_g[layer], w_ple_gate[layer], w_ple_proj[layer],
                   final_norm_g, batch=batch, seq=seq)
    return h.reshape(batch, seq, d_model)
```

```python
import jax
import jax.numpy as jnp
from jax import lax
from jax.experimental import pallas as pl
from jax.experimental.pallas import tpu as pltpu

F32 = jnp.float32
BF16 = jnp.bfloat16
EPS = 1e-6

HEADS = 8
HEAD_DIM = 128
SUBLANES = 8
SCAN_CHUNK = 64
SGU_CHUNK = 128
LEVELS = (1, 2, 4, 8, 16, 32)
VMEM_LIMIT = 60 * 1024 * 1024


def _dot(a, b):
    return jnp.dot(a, b, preferred_element_type=F32)


def _dot_nt(a, b):
    return lax.dot_general(a, b, (((1,), (1,)), ((), ())), preferred_element_type=F32)


def _dot_tn(a, b):
    return lax.dot_general(a, b, (((0,), (0,)), ((), ())), preferred_element_type=F32)


def _sigmoid(x):
    return 1.0 / (1.0 + jnp.exp(-x))


def _rms(x, g):
    return x * lax.rsqrt(jnp.mean(x * x, axis=-1, keepdims=True) + EPS) * g


def _resident(shape):
    nd = len(shape)
    return pl.BlockSpec(shape, lambda *_: (0,) * nd, pipeline_mode=pl.Buffered(1))


def _chunk_cumsum(x, reverse):
    rows, cols = x.shape
    x3 = x.reshape(rows // SUBLANES, SUBLANES, cols)
    sub = lax.broadcasted_iota(jnp.int32, x3.shape, 1)
    s = 1
    while s < SUBLANES:
        if reverse:
            x3 = x3 + jnp.where(sub < SUBLANES - s, pltpu.roll(x3, SUBLANES - s, axis=1), 0.0)
        else:
            x3 = x3 + jnp.where(sub >= s, pltpu.roll(x3, s, axis=1), 0.0)
        s *= 2
    groups = SCAN_CHUNK // SUBLANES
    edge = 0 if reverse else SUBLANES - 1
    out = [None] * (rows // SUBLANES)
    for c in range(rows // SCAN_CHUNK):
        carry = None
        order = range(groups - 1, -1, -1) if reverse else range(groups)
        for j in order:
            blk = x3[c * groups + j]
            total = blk[edge:edge + 1, :]
            out[c * groups + j] = blk if carry is None else blk + carry
            carry = total if carry is None else carry + total
    return jnp.concatenate(out, axis=0)


def _sgu(u_raw, v_raw, lng, lnb, sw_ref, sb_ref, sog, s_ref):
    def gelu(t):
        return 0.5 * t * (1.0 + lax.erf(t * (2.0 ** -0.5)))

    v = gelu(v_raw)
    vc = v - jnp.mean(v, axis=-1, keepdims=True)
    v = vc * lax.rsqrt(jnp.mean(vc * vc, axis=-1, keepdims=True) + EPS)
    v = (v * lng + lnb).astype(BF16)
    u = gelu(u_raw)
    for c in range(u.shape[0] // SGU_CHUNK):
        rows = slice(c * SGU_CHUNK, (c + 1) * SGU_CHUNK)
        mixed = [
            _dot(sw_ref[gi], v[rows, gi * HEAD_DIM:(gi + 1) * HEAD_DIM]) + sb_ref[gi]
            for gi in range(HEADS)
        ]
        s_ref[rows, :] = _rms(u[rows] * jnp.concatenate(mixed, axis=1), sog).astype(s_ref.dtype)


def _inproj_kernel(x_ref, gmix_ref, win_ref, lbf_ref, lbb_ref, lng_ref, lnb_ref,
                   sw_ref, sb_ref, sog_ref,
                   q_ref, kf_ref, kb_ref, pf_ref, cb_ref, v_ref, g_ref, s_ref):
    width = HEADS * HEAD_DIM
    xb = _rms(x_ref[...], gmix_ref[...]).astype(BF16)

    def proj(j):
        return _dot(xb, win_ref[:, j * width:(j + 1) * width])

    def put_heads(ref, val):
        for h in range(HEADS):
            ref[h] = val[:, h * HEAD_DIM:(h + 1) * HEAD_DIM].astype(ref.dtype)

    v_raw = proj(6)
    _sgu(proj(5), v_raw, lng_ref[...], lnb_ref[...], sw_ref, sb_ref, sog_ref[...], s_ref)

    def gates(j, lb_ref, k_ref, cum_ref, reverse):
        half = width // 2
        for part in range(2):
            cols = slice(part * half, (part + 1) * half)
            lb = lb_ref[:, cols]
            z = _dot(xb, win_ref[:, j * width + part * half:j * width + (part + 1) * half])
            sig = _sigmoid(z)
            cum = _chunk_cumsum(jnp.log2(lb + (1.0 - lb) * sig), reverse)
            k = (1.0 - lb) * (1.0 - sig)
            for h in range(HEADS // 2):
                hh = part * (HEADS // 2) + h
                cum_ref[hh] = cum[:, h * HEAD_DIM:(h + 1) * HEAD_DIM]
                k_ref[hh] = k[:, h * HEAD_DIM:(h + 1) * HEAD_DIM].astype(k_ref.dtype)

    gates(1, lbf_ref, kf_ref, pf_ref, False)
    q = proj(0)
    put_heads(q_ref, q * _sigmoid(q))
    gates(2, lbb_ref, kb_ref, cb_ref, True)
    g = proj(4)
    put_heads(g_ref, g * _sigmoid(g))
    put_heads(v_ref, proj(3))


def _inproj_call(x2, gmix, win, lbf, lbb, lng, lnb, sw, sb, sog, *, batch, seq, tm):
    tokens, d_model = x2.shape
    width = HEADS * HEAD_DIM
    nlb = seq // tm
    row = lambda a: _resident(a.shape)
    head_spec = pl.BlockSpec((None, HEADS, tm, HEAD_DIM), lambda i: (i // nlb, 0, i % nlb, 0))
    head_shape = lambda dt: jax.ShapeDtypeStruct((batch, HEADS, seq, HEAD_DIM), dt)
    return pl.pallas_call(
        _inproj_kernel,
        grid=(tokens // tm,),
        in_specs=[pl.BlockSpec((tm, d_model), lambda i: (i, 0)),
                  row(gmix), row(win), row(lbf), row(lbb), row(lng), row(lnb),
                  row(sw), row(sb), row(sog)],
        out_specs=[head_spec] * 7 + [pl.BlockSpec((tm, width), lambda i: (i, 0))],
        out_shape=[head_shape(BF16), head_shape(BF16), head_shape(BF16),
                   head_shape(F32), head_shape(F32), head_shape(BF16), head_shape(BF16),
                   jax.ShapeDtypeStruct((tokens, width), BF16)],
        compiler_params=pltpu.CompilerParams(
            dimension_semantics=("arbitrary",), vmem_limit_bytes=VMEM_LIMIT),
        name="inproj",
    )(x2, gmix, win, lbf, lbb, lng, lnb, sw, sb, sog)


def _chunk_masks():
    sub = lax.broadcasted_iota(jnp.int32, (SCAN_CHUNK, HEAD_DIM), 0) & (SUBLANES - 1)
    pair_xor = (lax.broadcasted_iota(jnp.int32, (SCAN_CHUNK, 2 * SCAN_CHUNK), 0)
                ^ (lax.broadcasted_iota(jnp.int32, (SCAN_CHUNK, 2 * SCAN_CHUNK), 1)
                   & (SCAN_CHUNK - 1)))
    late = {h: (sub & h) != 0 for h in LEVELS if h < SUBLANES}
    return late, sub < SUBLANES // 2, {h: pair_xor >= h for h in LEVELS}


def _sublane_ref(x, rows_per_group, r0, low_half):
    x3 = x.reshape(SCAN_CHUNK // SUBLANES, SUBLANES, HEAD_DIM)
    pick = lambda r: jnp.broadcast_to(x3[:, r:r + 1, :], x3.shape).reshape(x.shape)
    if rows_per_group == SUBLANES:
        return pick(r0)
    return jnp.where(low_half, pick(r0), pick(r0 + rows_per_group))


def _level_operands(q, kf, kb, pf, cb, h, late_rows, low_half):
    if h >= SUBLANES:
        lhs, rhs = [], []
        for r0 in range(0, SCAN_CHUNK, 2 * h):
            early, late = slice(r0, r0 + h), slice(r0 + h, r0 + 2 * h)
            ref_f = pf[r0 + h - 1:r0 + h, :]
            ref_b = cb[r0 + h:r0 + h + 1, :]
            lhs += [q[early] * jnp.exp2(cb[early] - ref_b), q[late] * jnp.exp2(pf[late] - ref_f)]
            rhs += [kf[early] * jnp.exp2(ref_f - pf[early]), kb[late] * jnp.exp2(ref_b - cb[late])]
        return jnp.concatenate(lhs, axis=0), jnp.concatenate(rhs, axis=0)
    late = late_rows[h]
    k_sel = jnp.where(late, kb, kf)
    if h == 1:
        d_f = pf - pltpu.roll(pf, 1, axis=0)
        d_b = cb - pltpu.roll(cb, SCAN_CHUNK - 1, axis=0)
        return q * jnp.exp2(jnp.where(late, d_f, d_b)), k_sel
    d_f = pf - _sublane_ref(pf, 2 * h, h - 1, low_half)
    d_b = cb - _sublane_ref(cb, 2 * h, h, low_half)
    lhs = q * jnp.exp2(jnp.where(late, d_f, d_b))
    rhs = k_sel * jnp.exp2(-jnp.where(late, d_b, d_f))
    return lhs, rhs


def _bwd_scan_kernel(kb_ref, cb_ref, v_ref, hist_ref, st_ref):
    heads = kb_ref.shape[0]
    nchunk = kb_ref.shape[1] // SCAN_CHUNK

    @pl.when(pl.program_id(1) == 0)
    def _():
        st_ref[...] = jnp.zeros_like(st_ref)

    def body(ci, carry):
        c = nchunk - 1 - ci
        rows = pl.ds(pl.multiple_of(c * SCAN_CHUNK, SCAN_CHUNK), SCAN_CHUNK)
        for h in range(heads):
            cb = cb_ref[h, rows, :]
            st = st_ref[h]
            hist_ref[h, c] = st.astype(hist_ref.dtype)
            tot = cb[0:1, :]
            kt = (kb_ref[h, rows, :].astype(F32) * jnp.exp2(tot - cb)).astype(BF16)
            st_ref[h] = st * jnp.exp2(tot) + _dot_tn(v_ref[h, rows, :], kt)
        return carry

    lax.fori_loop(0, nchunk, body, 0, unroll=8)


def _bwd_scan_call(kb, cb, v, *, tb):
    batch, heads, seq, _ = kb.shape
    nblk = seq // tb
    spec = pl.BlockSpec((None, heads, tb, HEAD_DIM), lambda b, i: (b, 0, nblk - 1 - i, 0))
    return pl.pallas_call(
        _bwd_scan_kernel,
        grid=(batch, nblk),
        in_specs=[spec, spec, spec],
        out_specs=pl.BlockSpec((None, heads, tb // SCAN_CHUNK, HEAD_DIM, HEAD_DIM),
                               lambda b, i: (b, 0, nblk - 1 - i, 0, 0)),
        out_shape=jax.ShapeDtypeStruct(
            (batch, heads, seq // SCAN_CHUNK, HEAD_DIM, HEAD_DIM), BF16),
        scratch_shapes=[pltpu.VMEM((heads, HEAD_DIM, HEAD_DIM), F32)],
        compiler_params=pltpu.CompilerParams(
            dimension_semantics=("arbitrary", "arbitrary"), vmem_limit_bytes=VMEM_LIMIT),
        name="hgrn_bwd_scan",
    )(kb, cb, v)


def _fwd_kernel(q_ref, kf_ref, kb_ref, pf_ref, cb_ref, v_ref, g_ref, hist_ref, og_ref,
                a_ref, st_ref, score_ref, part_ref):
    heads = q_ref.shape[0]
    nchunk = q_ref.shape[1] // SCAN_CHUNK

    @pl.when(pl.program_id(2) == 0)
    def _():
        st_ref[...] = jnp.zeros_like(st_ref)

    def chunk_rows(c):
        return pl.ds(pl.multiple_of(c * SCAN_CHUNK, SCAN_CHUNK), SCAN_CHUNK)

    zeros = jnp.zeros((SCAN_CHUNK, HEAD_DIM), BF16)

    def side_by_side(even, odd):
        return jnp.concatenate([even, odd], axis=1)

    def block_diag(even, odd):
        return jnp.concatenate([side_by_side(even, zeros), side_by_side(zeros, odd)], axis=0)

    def stage1(p, c, masks):
        rows, slot = chunk_rows(c), c % 2
        late_rows, low_half, _ = masks
        operands = []
        for h in (2 * p, 2 * p + 1):
            q = q_ref[h, rows, :].astype(F32)
            kf = kf_ref[h, rows, :].astype(F32)
            kb = kb_ref[h, rows, :].astype(F32)
            pf = pf_ref[h, rows, :]
            cb = cb_ref[h, rows, :]
            v = v_ref[h, rows, :]
            tiles = [_level_operands(q, kf, kb, pf, cb, lvl, late_rows, low_half)
                     for lvl in LEVELS]
            tiles.append((q, kf + kb))
            operands.append([(lhs.astype(BF16), rhs.astype(BF16)) for lhs, rhs in tiles])

            st = st_ref[h]
            q_in = jnp.concatenate([q * jnp.exp2(pf), q * jnp.exp2(cb)], axis=1)
            s_in = jnp.concatenate([st.astype(BF16), hist_ref[h, c]], axis=1)
            part_ref[slot, h] = _dot_nt(q_in.astype(BF16), s_in)

            tot = pf[SCAN_CHUNK - 1:SCAN_CHUNK, :]
            kt = (kf * jnp.exp2(tot - pf)).astype(BF16)
            st_ref[h] = st * jnp.exp2(tot) + _dot_tn(v, kt)
        for li, ((lhs_e, rhs_e), (lhs_o, rhs_o)) in enumerate(zip(*operands)):
            score_ref[slot, p, li] = _dot_nt(side_by_side(lhs_e, lhs_o), block_diag(rhs_e, rhs_o))

    def stage2(p, c, masks):
        rows, slot = chunk_rows(c), c % 2
        pair_level = masks[2]
        a = score_ref[slot, p, len(LEVELS)]
        for li, lvl in enumerate(LEVELS):
            a = jnp.where(pair_level[lvl], score_ref[slot, p, li], a)
        v_pair = block_diag(v_ref[2 * p, rows, :], v_ref[2 * p + 1, rows, :])
        o_pair = _dot(a.astype(BF16), v_pair)
        for j, h in enumerate((2 * p, 2 * p + 1)):
            o = o_pair[:, j * HEAD_DIM:(j + 1) * HEAD_DIM] + part_ref[slot, h]
            o = _rms(o, og_ref[...]) * g_ref[h, rows, :].astype(F32)
            a_ref[h, rows, :] = o.astype(a_ref.dtype)

    pairs = heads // 2
    masks0 = _chunk_masks()
    for p in range(pairs):
        stage1(p, 0, masks0)

    def body(c, carry):
        masks = _chunk_masks()
        for p in range(pairs):
            stage2(p, c - 1, masks)
        for p in range(pairs):
            stage1(p, c, masks)
        return carry

    lax.fori_loop(1, nchunk, body, 0)
    for p in range(pairs):
        stage2(p, nchunk - 1, masks0)


def _fwd_call(q, kf, kb, pf, cb, v, g, hist, og, *, tb, hp):
    batch, heads, seq, _ = q.shape
    spec = pl.BlockSpec((None, hp, tb, HEAD_DIM), lambda b, hg, i: (b, hg, i, 0))
    return pl.pallas_call(
        _fwd_kernel,
        grid=(batch, heads // hp, seq // tb),
        in_specs=[spec] * 7 + [
            pl.BlockSpec((None, hp, tb // SCAN_CHUNK, HEAD_DIM, HEAD_DIM),
                         lambda b, hg, i: (b, hg, i, 0, 0)),
            _resident(og.shape)],
        out_specs=spec,
        out_shape=jax.ShapeDtypeStruct((batch, heads, seq, HEAD_DIM), BF16),
        scratch_shapes=[pltpu.VMEM((hp, HEAD_DIM, HEAD_DIM), F32),
                        pltpu.VMEM((2, hp // 2, len(LEVELS) + 1, SCAN_CHUNK, 2 * SCAN_CHUNK), F32),
                        pltpu.VMEM((2, hp, SCAN_CHUNK, HEAD_DIM), F32)],
        compiler_params=pltpu.CompilerParams(
            dimension_semantics=("arbitrary", "arbitrary", "arbitrary"),
            vmem_limit_bytes=VMEM_LIMIT),
        name="hgrn_fwd",
    )(q, kf, kb, pf, cb, v, g, hist, og)


def _outproj_kernel(x_ref, a_ref, s_ref, wout_ref, gffn_ref, h_ref, hn_ref):
    mix = jnp.concatenate([a_ref[h] for h in range(HEADS)] + [s_ref[...]], axis=1)
    h = x_ref[...] + _dot(mix, wout_ref[...])
    h_ref[...] = h
    hn_ref[...] = _rms(h, gffn_ref[...]).astype(hn_ref.dtype)


def _outproj_call(x2, a, s, wout, gffn, *, seq, tm):
    tokens, d_model = x2.shape
    nlb = seq // tm
    tile = pl.BlockSpec((tm, d_model), lambda i: (i, 0))
    return pl.pallas_call(
        _outproj_kernel,
        grid=(tokens // tm,),
        in_specs=[tile,
                  pl.BlockSpec((None, HEADS, tm, HEAD_DIM), lambda i: (i // nlb, 0, i % nlb, 0)),
                  pl.BlockSpec((tm, s.shape[1]), lambda i: (i, 0)),
                  _resident(wout.shape), _resident(gffn.shape)],
        out_specs=[tile, tile],
        out_shape=[jax.ShapeDtypeStruct((tokens, d_model), F32),
                   jax.ShapeDtypeStruct((tokens, d_model), BF16)],
        compiler_params=pltpu.CompilerParams(
            dimension_semantics=("arbitrary",), vmem_limit_bytes=VMEM_LIMIT),
        name="outproj",
    )(x2, a, s, wout, gffn)


def _ffn_kernel(h_ref, hn_ref, p_ref, wg_ref, wu_ref, wd_ref, gple_ref, wpg_ref, wpp_ref,
                gout_ref, o_ref):
    j = pl.program_id(1)

    @pl.when(j == 0)
    def _():
        o_ref[...] = h_ref[...]

    hn = hn_ref[...]
    gate = _dot(hn, wg_ref[...])
    act = (gate * _sigmoid(gate) * _dot(hn, wu_ref[...])).astype(BF16)
    o_ref[...] += _dot(act, wd_ref[...])

    @pl.when(j == pl.num_programs(1) - 1)
    def _():
        h2 = o_ref[...]
        hp = _rms(h2, gple_ref[...]).astype(BF16)
        pgate = _sigmoid(_dot(hp, wpg_ref[...]))
        pproj = _dot(p_ref[...].astype(BF16), wpp_ref[...])
        o_ref[...] = _rms(h2 + pgate * pproj, gout_ref[...])


def _ffn_call(h, hn, p2, wg, wu, wd, gple, wpg, wpp, gout, *, tm, th):
    tokens, d_model = h.shape
    hidden = wg.shape[1]
    tile = pl.BlockSpec((tm, d_model), lambda i, j: (i, 0))
    return pl.pallas_call(
        _ffn_kernel,
        grid=(tokens // tm, hidden // th),
        in_specs=[tile, tile,
                  pl.BlockSpec((tm, p2.shape[1]), lambda i, j: (i, 0)),
                  pl.BlockSpec((d_model, th), lambda i, j: (0, j)),
                  pl.BlockSpec((d_model, th), lambda i, j: (0, j)),
                  pl.BlockSpec((th, d_model), lambda i, j: (j, 0)),
                  _resident(gple.shape), _resident(wpg.shape), _resident(wpp.shape),
                  _resident(gout.shape)],
        out_specs=tile,
        out_shape=jax.ShapeDtypeStruct((tokens, d_model), F32),
        compiler_params=pltpu.CompilerParams(
            dimension_semantics=("arbitrary", "arbitrary"), vmem_limit_bytes=VMEM_LIMIT),
        name="ffn_ple",
    )(h, hn, p2, wg, wu, wd, gple, wpg, wpp, gout)


def _layer(h2, p2, lb_f, lb_b, norm_mix_g, w_in, hgrn_onorm_g, sgu_ln_g, sgu_ln_b, sgu_w,
           sgu_b, sgu_onorm_g, w_out, norm_ffn_g, w_gate, w_up, w_down, norm_ple_g,
           w_ple_gate, w_ple_proj, out_g, *, batch, seq):
    row = lambda a: a.reshape(1, -1).astype(F32)
    q, kf, kb, pf, cb, v, g, s = _inproj_call(
        h2, row(norm_mix_g), w_in.astype(BF16), row(lb_f), row(lb_b), row(sgu_ln_g),
        row(sgu_ln_b), sgu_w.astype(BF16), sgu_b.astype(F32)[:, :, None], row(sgu_onorm_g),
        batch=batch, seq=seq, tm=256)
    hist = _bwd_scan_call(kb, cb, v, tb=512)
    a = _fwd_call(q, kf, kb, pf, cb, v, g, hist, row(hgrn_onorm_g), tb=1024, hp=8)
    hmid, hn = _outproj_call(h2, a, s, w_out.astype(BF16), row(norm_ffn_g), seq=seq, tm=512)
    return _ffn_call(hmid, hn, p2, w_gate.astype(BF16), w_up.astype(BF16), w_down.astype(BF16),
                     row(norm_ple_g), w_ple_gate.astype(BF16), w_ple_proj.astype(BF16),
                     row(out_g), tm=512, th=512)


def kernel(x, p, norm_mix_g, w_in, lb_fwd_logits, lb_bwd_logits, hgrn_onorm_g, sgu_ln_g, sgu_ln_b, sgu_w, sgu_b, sgu_onorm_g, w_out, norm_ffn_g, w_gate, w_up, w_down, norm_ple_g, w_ple_gate, w_ple_proj, final_norm_g):
    batch, seq, d_model = x.shape
    depth = w_in.shape[0]
    lb_f_all = jnp.cumsum(jax.nn.softmax(lb_fwd_logits.astype(F32), axis=0), axis=0)
    lb_b_all = jnp.cumsum(jax.nn.softmax(lb_bwd_logits.astype(F32), axis=0), axis=0)
    h = x.reshape(batch * seq, d_model)
    for layer in range(depth):
        assert layer == depth - 1
        h = _layer(h, p[layer].reshape(batch * seq, -1), lb_f_all[layer], lb_b_all[layer],
                   norm_mix_g[layer], w_in[layer], hgrn_onorm_g[layer], sgu_ln_g[layer],
                   sgu_ln_b[layer], sgu_w[layer], sgu_b[layer], sgu_onorm_g[layer],
                   w_out[layer], norm_ffn_g[layer], w_gate[layer], w_up[layer],
                   w_down[layer], norm_ple_g[layer], w_ple_gate[layer], w_ple_proj[layer],
                   final_norm_g, batch=batch, seq=seq)
    return h.reshape(batch, seq, d_model)
```

```python
import jax
import jax.numpy as jnp
from jax import lax
from jax.experimental import pallas as pl
from jax.experimental.pallas import tpu as pltpu

F32 = jnp.float32
BF16 = jnp.bfloat16
EPS = 1e-6

HEADS = 8
HEAD_DIM = 128
SUBLANES = 8
SCAN_CHUNK = 64
SGU_CHUNK = 128
LEVELS = (1, 2, 4, 8, 16, 32)
VMEM_LIMIT = 60 * 1024 * 1024


def _dot(a, b):
    return jnp.dot(a, b, preferred_element_type=F32)


def _dot_nt(a, b):
    return lax.dot_general(a, b, (((1,), (1,)), ((), ())), preferred_element_type=F32)


def _dot_tn(a, b):
    return lax.dot_general(a, b, (((0,), (0,)), ((), ())), preferred_element_type=F32)


def _sigmoid(x):
    return 1.0 / (1.0 + jnp.exp(-x))


def _rms(x, g):
    return x * lax.rsqrt(jnp.mean(x * x, axis=-1, keepdims=True) + EPS) * g


def _resident(shape):
    nd = len(shape)
    return pl.BlockSpec(shape, lambda *_: (0,) * nd, pipeline_mode=pl.Buffered(1))


def _chunk_cumsum(x, reverse):
    rows, cols = x.shape
    x3 = x.reshape(rows // SUBLANES, SUBLANES, cols)
    sub = lax.broadcasted_iota(jnp.int32, x3.shape, 1)
    s = 1
    while s < SUBLANES:
        if reverse:
            x3 = x3 + jnp.where(sub < SUBLANES - s, pltpu.roll(x3, SUBLANES - s, axis=1), 0.0)
        else:
            x3 = x3 + jnp.where(sub >= s, pltpu.roll(x3, s, axis=1), 0.0)
        s *= 2
    groups = SCAN_CHUNK // SUBLANES
    edge = 0 if reverse else SUBLANES - 1
    out = [None] * (rows // SUBLANES)
    for c in range(rows // SCAN_CHUNK):
        carry = None
        order = range(groups - 1, -1, -1) if reverse else range(groups)
        for j in order:
            blk = x3[c * groups + j]
            total = blk[edge:edge + 1, :]
            out[c * groups + j] = blk if carry is None else blk + carry
            carry = total if carry is None else carry + total
    return jnp.concatenate(out, axis=0)


def _sgu(u_raw, v_raw, lng, lnb, sw_ref, sb_ref, sog, s_ref):
    def gelu(t):
        return 0.5 * t * (1.0 + lax.erf(t * (2.0 ** -0.5)))

    v = gelu(v_raw)
    vc = v - jnp.mean(v, axis=-1, keepdims=True)
    v = vc * lax.rsqrt(jnp.mean(vc * vc, axis=-1, keepdims=True) + EPS)
    v = (v * lng + lnb).astype(BF16)
    u = gelu(u_raw)
    for c in range(u.shape[0] // SGU_CHUNK):
        rows = slice(c * SGU_CHUNK, (c + 1) * SGU_CHUNK)
        mixed = [
            _dot(sw_ref[gi], v[rows, gi * HEAD_DIM:(gi + 1) * HEAD_DIM]) + sb_ref[gi]
            for gi in range(HEADS)
        ]
        s_ref[rows, :] = _rms(u[rows] * jnp.concatenate(mixed, axis=1), sog).astype(s_ref.dtype)


def _inproj_kernel(x_ref, gmix_ref, win_ref, lbf_ref, lbb_ref, lng_ref, lnb_ref,
                   sw_ref, sb_ref, sog_ref,
                   q_ref, kf_ref, kb_ref, pf_ref, cb_ref, v_ref, g_ref, s_ref):
    width = HEADS * HEAD_DIM
    xb = _rms(x_ref[...], gmix_ref[...]).astype(BF16)

    def proj(j):
        return _dot(xb, win_ref[:, j * width:(j + 1) * width])

    def put_heads(ref, val):
        for h in range(HEADS):
            ref[h] = val[:, h * HEAD_DIM:(h + 1) * HEAD_DIM].astype(ref.dtype)

    v_raw = proj(6)
    _sgu(proj(5), v_raw, lng_ref[...], lnb_ref[...], sw_ref, sb_ref, sog_ref[...], s_ref)

    def gates(j, lb_ref, k_ref, cum_ref, reverse):
        half = width // 2
        for part in range(2):
            cols = slice(part * half, (part + 1) * half)
            lb = lb_ref[:, cols]
            z = _dot(xb, win_ref[:, j * width + part * half:j * width + (part + 1) * half])
            sig = _sigmoid(z)
            cum = _chunk_cumsum(jnp.log2(lb + (1.0 - lb) * sig), reverse)
            k = (1.0 - lb) * (1.0 - sig)
            for h in range(HEADS // 2):
                hh = part * (HEADS // 2) + h
                cum_ref[hh] = cum[:, h * HEAD_DIM:(h + 1) * HEAD_DIM]
                k_ref[hh] = k[:, h * HEAD_DIM:(h + 1) * HEAD_DIM].astype(k_ref.dtype)

    gates(1, lbf_ref, kf_ref, pf_ref, False)
    q = proj(0)
    put_heads(q_ref, q * _sigmoid(q))
    gates(2, lbb_ref, kb_ref, cb_ref, True)
    g = proj(4)
    put_heads(g_ref, g * _sigmoid(g))
    put_heads(v_ref, proj(3))


def _inproj_call(x2, gmix, win, lbf, lbb, lng, lnb, sw, sb, sog, *, batch, seq, tm):
    tokens, d_model = x2.shape
    width = HEADS * HEAD_DIM
    nlb = seq // tm
    row = lambda a: _resident(a.shape)
    head_spec = pl.BlockSpec((None, HEADS, tm, HEAD_DIM), lambda i: (i // nlb, 0, i % nlb, 0))
    head_shape = lambda dt: jax.ShapeDtypeStruct((batch, HEADS, seq, HEAD_DIM), dt)
    return pl.pallas_call(
        _inproj_kernel,
        grid=(tokens // tm,),
        in_specs=[pl.BlockSpec((tm, d_model), lambda i: (i, 0)),
                  row(gmix), row(win), row(lbf), row(lbb), row(lng), row(lnb),
                  row(sw), row(sb), row(sog)],
        out_specs=[head_spec] * 7 + [pl.BlockSpec((tm, width), lambda i: (i, 0))],
        out_shape=[head_shape(BF16), head_shape(BF16), head_shape(BF16),
                   head_shape(F32), head_shape(F32), head_shape(BF16), head_shape(BF16),
                   jax.ShapeDtypeStruct((tokens, width), BF16)],
        compiler_params=pltpu.CompilerParams(
            dimension_semantics=("arbitrary",), vmem_limit_bytes=VMEM_LIMIT),
        name="inproj",
    )(x2, gmix, win, lbf, lbb, lng, lnb, sw, sb, sog)


def _chunk_masks():
    sub = lax.broadcasted_iota(jnp.int32, (SCAN_CHUNK, HEAD_DIM), 0) & (SUBLANES - 1)
    pair_xor = (lax.broadcasted_iota(jnp.int32, (SCAN_CHUNK, 2 * SCAN_CHUNK), 0)
                ^ (lax.broadcasted_iota(jnp.int32, (SCAN_CHUNK, 2 * SCAN_CHUNK), 1)
                   & (SCAN_CHUNK - 1)))
    late = {h: (sub & h) != 0 for h in LEVELS if h < SUBLANES}
    return late, sub < SUBLANES // 2, {h: pair_xor >= h for h in LEVELS}


def _sublane_ref(x, rows_per_group, r0, low_half):
    x3 = x.reshape(SCAN_CHUNK // SUBLANES, SUBLANES, HEAD_DIM)
    pick = lambda r: jnp.broadcast_to(x3[:, r:r + 1, :], x3.shape).reshape(x.shape)
    if rows_per_group == SUBLANES:
        return pick(r0)
    return jnp.where(low_half, pick(r0), pick(r0 + rows_per_group))


def _level_operands(q, kf, kb, pf, cb, h, late_rows, low_half):
    if h >= SUBLANES:
        lhs, rhs = [], []
        for r0 in range(0, SCAN_CHUNK, 2 * h):
            early, late = slice(r0, r0 + h), slice(r0 + h, r0 + 2 * h)
            ref_f = pf[r0 + h - 1:r0 + h, :]
            ref_b = cb[r0 + h:r0 + h + 1, :]
            lhs += [q[early] * jnp.exp2(cb[early] - ref_b), q[late] * jnp.exp2(pf[late] - ref_f)]
            rhs += [kf[early] * jnp.exp2(ref_f - pf[early]), kb[late] * jnp.exp2(ref_b - cb[late])]
        return jnp.concatenate(lhs, axis=0), jnp.concatenate(rhs, axis=0)
    late = late_rows[h]
    k_sel = jnp.where(late, kb, kf)
    if h == 1:
        d_f = pf - pltpu.roll(pf, 1, axis=0)
        d_b = cb - pltpu.roll(cb, SCAN_CHUNK - 1, axis=0)
        return q * jnp.exp2(jnp.where(late, d_f, d_b)), k_sel
    d_f = pf - _sublane_ref(pf, 2 * h, h - 1, low_half)
    d_b = cb - _sublane_ref(cb, 2 * h, h, low_half)
    lhs = q * jnp.exp2(jnp.where(late, d_f, d_b))
    rhs = k_sel * jnp.exp2(-jnp.where(late, d_b, d_f))
    return lhs, rhs


def _bwd_scan_kernel(kb_ref, cb_ref, v_ref, hist_ref, st_ref):
    heads = kb_ref.shape[0]
    nchunk = kb_ref.shape[1] // SCAN_CHUNK

    @pl.when(pl.program_id(1) == 0)
    def _():
        st_ref[...] = jnp.zeros_like(st_ref)

    def body(ci, carry):
        c = nchunk - 1 - ci
        rows = pl.ds(pl.multiple_of(c * SCAN_CHUNK, SCAN_CHUNK), SCAN_CHUNK)
        for h in range(heads):
            cb = cb_ref[h, rows, :]
            st = st_ref[h]
            hist_ref[h, c] = st.astype(hist_ref.dtype)
            tot = cb[0:1, :]
            kt = (kb_ref[h, rows, :].astype(F32) * jnp.exp2(tot - cb)).astype(BF16)
            st_ref[h] = st * jnp.exp2(tot) + _dot_tn(v_ref[h, rows, :], kt)
        return carry

    lax.fori_loop(0, nchunk, body, 0, unroll=8)


def _bwd_scan_call(kb, cb, v, *, tb):
    batch, heads, seq, _ = kb.shape
    nblk = seq // tb
    spec = pl.BlockSpec((None, heads, tb, HEAD_DIM), lambda b, i: (b, 0, nblk - 1 - i, 0))
    return pl.pallas_call(
        _bwd_scan_kernel,
        grid=(batch, nblk),
        in_specs=[spec, spec, spec],
        out_specs=pl.BlockSpec((None, heads, tb // SCAN_CHUNK, HEAD_DIM, HEAD_DIM),
                               lambda b, i: (b, 0, nblk - 1 - i, 0, 0)),
        out_shape=jax.ShapeDtypeStruct(
            (batch, heads, seq // SCAN_CHUNK, HEAD_DIM, HEAD_DIM), BF16),
        scratch_shapes=[pltpu.VMEM((heads, HEAD_DIM, HEAD_DIM), F32)],
        compiler_params=pltpu.CompilerParams(
            dimension_semantics=("arbitrary", "arbitrary"), vmem_limit_bytes=VMEM_LIMIT),
        name="hgrn_bwd_scan",
    )(kb, cb, v)


def _fwd_kernel(q_ref, kf_ref, kb_ref, pf_ref, cb_ref, v_ref, g_ref, hist_ref, og_ref,
                a_ref, st_ref, score_ref, part_ref):
    heads = q_ref.shape[0]
    nchunk = q_ref.shape[1] // SCAN_CHUNK

    @pl.when(pl.program_id(2) == 0)
    def _():
        st_ref[...] = jnp.zeros_like(st_ref)

    def chunk_rows(c):
        return pl.ds(pl.multiple_of(c * SCAN_CHUNK, SCAN_CHUNK), SCAN_CHUNK)

    zeros = jnp.zeros((SCAN_CHUNK, HEAD_DIM), BF16)

    def side_by_side(even, odd):
        return jnp.concatenate([even, odd], axis=1)

    def block_diag(even, odd):
        return jnp.concatenate([side_by_side(even, zeros), side_by_side(zeros, odd)], axis=0)

    def stage1(p, c, masks):
        rows, slot = chunk_rows(c), c % 2
        late_rows, low_half, _ = masks
        operands = []
        for h in (2 * p, 2 * p + 1):
            q = q_ref[h, rows, :].astype(F32)
            kf = kf_ref[h, rows, :].astype(F32)
            kb = kb_ref[h, rows, :].astype(F32)
            pf = pf_ref[h, rows, :]
            cb = cb_ref[h, rows, :]
            v = v_ref[h, rows, :]
            tiles = [_level_operands(q, kf, kb, pf, cb, lvl, late_rows, low_half)
                     for lvl in LEVELS]
            tiles.append((q, kf + kb))
            operands.append([(lhs.astype(BF16), rhs.astype(BF16)) for lhs, rhs in tiles])

            st = st_ref[h]
            q_in = jnp.concatenate([q * jnp.exp2(pf), q * jnp.exp2(cb)], axis=1)
            s_in = jnp.concatenate([st.astype(BF16), hist_ref[h, c]], axis=1)
            part_ref[slot, h] = _dot_nt(q_in.astype(BF16), s_in)

            tot = pf[SCAN_CHUNK - 1:SCAN_CHUNK, :]
            kt = (kf * jnp.exp2(tot - pf)).astype(BF16)
            st_ref[h] = st * jnp.exp2(tot) + _dot_tn(v, kt)
        for li, ((lhs_e, rhs_e), (lhs_o, rhs_o)) in enumerate(zip(*operands)):
            score_ref[slot, p, li] = _dot_nt(side_by_side(lhs_e, lhs_o), block_diag(rhs_e, rhs_o))

    def stage2(p, c, masks):
        rows, slot = chunk_rows(c), c % 2
        pair_level = masks[2]
        a = score_ref[slot, p, len(LEVELS)]
        for li, lvl in enumerate(LEVELS):
            a = jnp.where(pair_level[lvl], score_ref[slot, p, li], a)
        v_pair = block_diag(v_ref[2 * p, rows, :], v_ref[2 * p + 1, rows, :])
        o_pair = _dot(a.astype(BF16), v_pair)
        for j, h in enumerate((2 * p, 2 * p + 1)):
            o = o_pair[:, j * HEAD_DIM:(j + 1) * HEAD_DIM] + part_ref[slot, h]
            o = _rms(o, og_ref[...]) * g_ref[h, rows, :].astype(F32)
            a_ref[h, rows, :] = o.astype(a_ref.dtype)

    pairs = heads // 2
    masks0 = _chunk_masks()
    for p in range(pairs):
        stage1(p, 0, masks0)

    def body(c, carry):
        masks = _chunk_masks()
        for p in range(pairs):
            stage2(p, c - 1, masks)
        for p in range(pairs):
            stage1(p, c, masks)
        return carry

    lax.fori_loop(1, nchunk, body, 0)
    for p in range(pairs):
        stage2(p, nchunk - 1, masks0)


def _fwd_call(q, kf, kb, pf, cb, v, g, hist, og, *, tb, hp):
    batch, heads, seq, _ = q.shape
    spec = pl.BlockSpec((None, hp, tb, HEAD_DIM), lambda b, hg, i: (b, hg, i, 0))
    return pl.pallas_call(
        _fwd_kernel,
        grid=(batch, heads // hp, seq // tb),
        in_specs=[spec] * 7 + [
            pl.BlockSpec((None, hp, tb // SCAN_CHUNK, HEAD_DIM, HEAD_DIM),
                         lambda b, hg, i: (b, hg, i, 0, 0)),
            _resident(og.shape)],
        out_specs=spec,
        out_shape=jax.ShapeDtypeStruct((batch, heads, seq, HEAD_DIM), BF16),
        scratch_shapes=[pltpu.VMEM((hp, HEAD_DIM, HEAD_DIM), F32),
                        pltpu.VMEM((2, hp // 2, len(LEVELS) + 1, SCAN_CHUNK, 2 * SCAN_CHUNK), F32),
                        pltpu.VMEM((2, hp, SCAN_CHUNK, HEAD_DIM), F32)],
        compiler_params=pltpu.CompilerParams(
            dimension_semantics=("arbitrary", "arbitrary", "arbitrary"),
            vmem_limit_bytes=VMEM_LIMIT),
        name="hgrn_fwd",
    )(q, kf, kb, pf, cb, v, g, hist, og)


def _outproj_kernel(x_ref, a_ref, s_ref, wout_ref, gffn_ref, h_ref, hn_ref):
    mix = jnp.concatenate([a_ref[h] for h in range(HEADS)] + [s_ref[...]], axis=1)
    h = x_ref[...] + _dot(mix, wout_ref[...])
    h_ref[...] = h
    hn_ref[...] = _rms(h, gffn_ref[...]).astype(hn_ref.dtype)


def _outproj_call(x2, a, s, wout, gffn, *, seq, tm):
    tokens, d_model = x2.shape
    nlb = seq // tm
    tile = pl.BlockSpec((tm, d_model), lambda i: (i, 0))
    return pl.pallas_call(
        _outproj_kernel,
        grid=(tokens // tm,),
        in_specs=[tile,
                  pl.BlockSpec((None, HEADS, tm, HEAD_DIM), lambda i: (i // nlb, 0, i % nlb, 0)),
                  pl.BlockSpec((tm, s.shape[1]), lambda i: (i, 0)),
                  _resident(wout.shape), _resident(gffn.shape)],
        out_specs=[tile, tile],
        out_shape=[jax.ShapeDtypeStruct((tokens, d_model), F32),
                   jax.ShapeDtypeStruct((tokens, d_model), BF16)],
        compiler_params=pltpu.CompilerParams(
            dimension_semantics=("arbitrary",), vmem_limit_bytes=VMEM_LIMIT),
        name="outproj",
    )(x2, a, s, wout, gffn)


def _ffn_kernel(hn_ref, wg_ref, wu_ref, wd_ref, o_ref):
    @pl.when(pl.program_id(1) == 0)
    def _():
        o_ref[...] = jnp.zeros_like(o_ref)

    hn = hn_ref[...]
    gate = _dot(hn, wg_ref[...])
    act = (gate * _sigmoid(gate) * _dot(hn, wu_ref[...])).astype(BF16)
    o_ref[...] += _dot(act, wd_ref[...])


def _ffn_call(hn, wg, wu, wd, *, tm, th):
    tokens, d_model = hn.shape
    hidden = wg.shape[1]
    tile = pl.BlockSpec((tm, d_model), lambda i, j: (i, 0))
    return pl.pallas_call(
        _ffn_kernel,
        grid=(tokens // tm, hidden // th),
        in_specs=[tile,
                  pl.BlockSpec((d_model, th), lambda i, j: (0, j)),
                  pl.BlockSpec((d_model, th), lambda i, j: (0, j)),
                  pl.BlockSpec((th, d_model), lambda i, j: (j, 0))],
        out_specs=tile,
        out_shape=jax.ShapeDtypeStruct((tokens, d_model), F32),
        compiler_params=pltpu.CompilerParams(
            dimension_semantics=("arbitrary", "arbitrary"), vmem_limit_bytes=VMEM_LIMIT),
        name="ffn",
    )(hn, wg, wu, wd)


def _ple_kernel(h_ref, f_ref, p_ref, gple_ref, wpg_ref, wpp_ref, gout_ref, o_ref):
    h2 = h_ref[...] + f_ref[...]
    hp = _rms(h2, gple_ref[...]).astype(BF16)
    pgate = _sigmoid(_dot(hp, wpg_ref[...]))
    pproj = _dot(p_ref[...].astype(BF16), wpp_ref[...])
    o_ref[...] = _rms(h2 + pgate * pproj, gout_ref[...])


def _ple_call(h, f, p2, gple, wpg, wpp, gout, *, tm):
    tokens, d_model = h.shape
    tile = pl.BlockSpec((tm, d_model), lambda i: (i, 0))
    return pl.pallas_call(
        _ple_kernel,
        grid=(tokens // tm,),
        in_specs=[tile, tile, pl.BlockSpec((tm, p2.shape[1]), lambda i: (i, 0)),
                  _resident(gple.shape), _resident(wpg.shape), _resident(wpp.shape),
                  _resident(gout.shape)],
        out_specs=tile,
        out_shape=jax.ShapeDtypeStruct((tokens, d_model), F32),
        compiler_params=pltpu.CompilerParams(
            dimension_semantics=("arbitrary",), vmem_limit_bytes=VMEM_LIMIT),
        name="ple_final",
    )(h, f, p2, gple, wpg, wpp, gout)


def _layer(h2, p2, lb_f, lb_b, norm_mix_g, w_in, hgrn_onorm_g, sgu_ln_g, sgu_ln_b, sgu_w,
           sgu_b, sgu_onorm_g, w_out, norm_ffn_g, w_gate, w_up, w_down, norm_ple_g,
           w_ple_gate, w_ple_proj, out_g, *, batch, seq):
    row = lambda a: a.reshape(1, -1).astype(F32)
    q, kf, kb, pf, cb, v, g, s = _inproj_call(
        h2, row(norm_mix_g), w_in.astype(BF16), row(lb_f), row(lb_b), row(sgu_ln_g),
        row(sgu_ln_b), sgu_w.astype(BF16), sgu_b.astype(F32)[:, :, None], row(sgu_onorm_g),
        batch=batch, seq=seq, tm=256)
    hist = _bwd_scan_call(kb, cb, v, tb=512)
    a = _fwd_call(q, kf, kb, pf, cb, v, g, hist, row(hgrn_onorm_g), tb=1024, hp=8)
    hmid, hn = _outproj_call(h2, a, s, w_out.astype(BF16), row(norm_ffn_g), seq=seq, tm=512)
    ffn = _ffn_call(hn, w_gate.astype(BF16), w_up.astype(BF16), w_down.astype(BF16),
                    tm=1024, th=512)
    return _ple_call(hmid, ffn, p2, row(norm_ple_g), w_ple_gate.astype(BF16),
                     w_ple_proj.astype(BF16), row(out_g), tm=512)


def kernel(x, p, norm_mix_g, w_in, lb_fwd_logits, lb_bwd_logits, hgrn_onorm_g, sgu_ln_g, sgu_ln_b, sgu_w, sgu_b, sgu_onorm_g, w_out, norm_ffn_g, w_gate, w_up, w_down, norm_ple_g, w_ple_gate, w_ple_proj, final_norm_g):
    batch, seq, d_model = x.shape
    depth = w_in.shape[0]
    lb_f_all = jnp.cumsum(jax.nn.softmax(lb_fwd_logits.astype(F32), axis=0), axis=0)
    lb_b_all = jnp.cumsum(jax.nn.softmax(lb_bwd_logits.astype(F32), axis=0), axis=0)
    h = x.reshape(batch * seq, d_model)
    for layer in range(depth):
        assert layer == depth - 1
        h = _layer(h, p[layer].reshape(batch * seq, -1), lb_f_all[layer], lb_b_all[layer],
                   norm_mix_g[layer], w_in[layer], hgrn_onorm_g[layer], sgu_ln_g[layer],
                   sgu_ln_b[layer], sgu_w[layer], sgu_b[layer], sgu_onorm_g[layer],
                   w_out[layer], norm_ffn_g[layer], w_gate[layer], w_up[layer],
                   w_down[layer], norm_ple_g[layer], w_ple_gate[layer], w_ple_proj[layer],
                   final_norm_g, batch=batch, seq=seq)
    return h.reshape(batch, seq, d_model)
```

```python
import jax
import jax.numpy as jnp
from jax import lax
from jax.experimental import pallas as pl
from jax.experimental.pallas import tpu as pltpu

F32 = jnp.float32
BF16 = jnp.bfloat16
EPS = 1e-6

HEADS = 8
HEAD_DIM = 128
SUBLANES = 8
SCAN_CHUNK = 64
SGU_CHUNK = 128
LEVELS = (1, 2, 4, 8, 16, 32)
VMEM_LIMIT = 60 * 1024 * 1024


def _dot(a, b):
    return jnp.dot(a, b, preferred_element_type=F32)


def _dot_nt(a, b):
    return lax.dot_general(a, b, (((1,), (1,)), ((), ())), preferred_element_type=F32)


def _dot_tn(a, b):
    return lax.dot_general(a, b, (((0,), (0,)), ((), ())), preferred_element_type=F32)


def _sigmoid(x):
    return 1.0 / (1.0 + jnp.exp(-x))


def _rms(x, g):
    return x * lax.rsqrt(jnp.mean(x * x, axis=-1, keepdims=True) + EPS) * g


def _resident(shape):
    nd = len(shape)
    return pl.BlockSpec(shape, lambda *_: (0,) * nd, pipeline_mode=pl.Buffered(1))


def _chunk_cumsum(x, reverse):
    rows, cols = x.shape
    x3 = x.reshape(rows // SUBLANES, SUBLANES, cols)
    sub = lax.broadcasted_iota(jnp.int32, x3.shape, 1)
    s = 1
    while s < SUBLANES:
        if reverse:
            x3 = x3 + jnp.where(sub < SUBLANES - s, pltpu.roll(x3, SUBLANES - s, axis=1), 0.0)
        else:
            x3 = x3 + jnp.where(sub >= s, pltpu.roll(x3, s, axis=1), 0.0)
        s *= 2
    groups = SCAN_CHUNK // SUBLANES
    edge = 0 if reverse else SUBLANES - 1
    out = [None] * (rows // SUBLANES)
    for c in range(rows // SCAN_CHUNK):
        carry = None
        order = range(groups - 1, -1, -1) if reverse else range(groups)
        for j in order:
            blk = x3[c * groups + j]
            total = blk[edge:edge + 1, :]
            out[c * groups + j] = blk if carry is None else blk + carry
            carry = total if carry is None else carry + total
    return jnp.concatenate(out, axis=0)


def _sgu(u_raw, v_raw, lng, lnb, sw_ref, sb_ref, sog, s_ref):
    def gelu(t):
        return 0.5 * t * (1.0 + lax.erf(t * (2.0 ** -0.5)))

    v = gelu(v_raw)
    vc = v - jnp.mean(v, axis=-1, keepdims=True)
    v = vc * lax.rsqrt(jnp.mean(vc * vc, axis=-1, keepdims=True) + EPS)
    v = (v * lng + lnb).astype(BF16)
    u = gelu(u_raw)
    for c in range(u.shape[0] // SGU_CHUNK):
        rows = slice(c * SGU_CHUNK, (c + 1) * SGU_CHUNK)
        mixed = [
            _dot(sw_ref[gi], v[rows, gi * HEAD_DIM:(gi + 1) * HEAD_DIM]) + sb_ref[gi]
            for gi in range(HEADS)
        ]
        s_ref[rows, :] = _rms(u[rows] * jnp.concatenate(mixed, axis=1), sog).astype(s_ref.dtype)


def _inproj_kernel(x_ref, gmix_ref, win_ref, lbf_ref, lbb_ref, lng_ref, lnb_ref,
                   sw_ref, sb_ref, sog_ref,
                   q_ref, kf_ref, kb_ref, pf_ref, cb_ref, v_ref, g_ref, s_ref):
    width = HEADS * HEAD_DIM
    xb = _rms(x_ref[...], gmix_ref[...]).astype(BF16)

    def proj(j):
        return _dot(xb, win_ref[:, j * width:(j + 1) * width])

    def put_heads(ref, val):
        for h in range(HEADS):
            ref[h] = val[:, h * HEAD_DIM:(h + 1) * HEAD_DIM].astype(ref.dtype)

    v_raw = proj(6)
    _sgu(proj(5), v_raw, lng_ref[...], lnb_ref[...], sw_ref, sb_ref, sog_ref[...], s_ref)

    def gates(j, lb_ref, k_ref, cum_ref, reverse):
        half = width // 2
        for part in range(2):
            cols = slice(part * half, (part + 1) * half)
            lb = lb_ref[:, cols]
            z = _dot(xb, win_ref[:, j * width + part * half:j * width + (part + 1) * half])
            sig = _sigmoid(z)
            cum = _chunk_cumsum(jnp.log2(lb + (1.0 - lb) * sig), reverse)
            k = (1.0 - lb) * (1.0 - sig)
            for h in range(HEADS // 2):
                hh = part * (HEADS // 2) + h
                cum_ref[hh] = cum[:, h * HEAD_DIM:(h + 1) * HEAD_DIM]
                k_ref[hh] = k[:, h * HEAD_DIM:(h + 1) * HEAD_DIM].astype(k_ref.dtype)

    gates(1, lbf_ref, kf_ref, pf_ref, False)
    q = proj(0)
    put_heads(q_ref, q * _sigmoid(q))
    gates(2, lbb_ref, kb_ref, cb_ref, True)
    g = proj(4)
    put_heads(g_ref, g * _sigmoid(g))
    put_heads(v_ref, proj(3))


def _inproj_call(x2, gmix, win, lbf, lbb, lng, lnb, sw, sb, sog, *, batch, seq, tm):
    tokens, d_model = x2.shape
    width = HEADS * HEAD_DIM
    nlb = seq // tm
    row = lambda a: _resident(a.shape)
    head_spec = pl.BlockSpec((None, HEADS, tm, HEAD_DIM), lambda i: (i // nlb, 0, i % nlb, 0))
    head_shape = lambda dt: jax.ShapeDtypeStruct((batch, HEADS, seq, HEAD_DIM), dt)
    return pl.pallas_call(
        _inproj_kernel,
        grid=(tokens // tm,),
        in_specs=[pl.BlockSpec((tm, d_model), lambda i: (i, 0)),
                  row(gmix), row(win), row(lbf), row(lbb), row(lng), row(lnb),
                  row(sw), row(sb), row(sog)],
        out_specs=[head_spec] * 7 + [pl.BlockSpec((tm, width), lambda i: (i, 0))],
        out_shape=[head_shape(BF16), head_shape(BF16), head_shape(BF16),
                   head_shape(F32), head_shape(F32), head_shape(BF16), head_shape(BF16),
                   jax.ShapeDtypeStruct((tokens, width), BF16)],
        compiler_params=pltpu.CompilerParams(
            dimension_semantics=("arbitrary",), vmem_limit_bytes=VMEM_LIMIT),
        name="inproj",
    )(x2, gmix, win, lbf, lbb, lng, lnb, sw, sb, sog)


def _chunk_masks():
    sub = lax.broadcasted_iota(jnp.int32, (SCAN_CHUNK, HEAD_DIM), 0) & (SUBLANES - 1)
    pair_xor = (lax.broadcasted_iota(jnp.int32, (SCAN_CHUNK, 2 * SCAN_CHUNK), 0)
                ^ (lax.broadcasted_iota(jnp.int32, (SCAN_CHUNK, 2 * SCAN_CHUNK), 1)
                   & (SCAN_CHUNK - 1)))
    late = {h: (sub & h) != 0 for h in LEVELS if h < SUBLANES}
    return late, sub < SUBLANES // 2, {h: pair_xor >= h for h in LEVELS}


def _sublane_ref(row, rows_per_group, r0, low_half):
    def pick(r):
        return jnp.concatenate(
            [jnp.broadcast_to(row(j * SUBLANES + r), (SUBLANES, HEAD_DIM))
             for j in range(SCAN_CHUNK // SUBLANES)], axis=0)
    if rows_per_group == SUBLANES:
        return pick(r0)
    return jnp.where(low_half, pick(r0), pick(r0 + rows_per_group))


def _level_operands(q, kf, kb, pf, cb, pf_row, cb_row, h, late_rows, low_half):
    if h >= SUBLANES:
        lhs, rhs = [], []
        for r0 in range(0, SCAN_CHUNK, 2 * h):
            early, late = slice(r0, r0 + h), slice(r0 + h, r0 + 2 * h)
            ref_f = pf_row(r0 + h - 1)
            ref_b = cb_row(r0 + h)
            lhs += [q[early] * jnp.exp2(cb[early] - ref_b), q[late] * jnp.exp2(pf[late] - ref_f)]
            rhs += [kf[early] * jnp.exp2(ref_f - pf[early]), kb[late] * jnp.exp2(ref_b - cb[late])]
        return jnp.concatenate(lhs, axis=0), jnp.concatenate(rhs, axis=0)
    late = late_rows[h]
    k_sel = jnp.where(late, kb, kf)
    if h == 1:
        d_f = pf - pltpu.roll(pf, 1, axis=0)
        d_b = cb - pltpu.roll(cb, SCAN_CHUNK - 1, axis=0)
        return q * jnp.exp2(jnp.where(late, d_f, d_b)), k_sel
    d_f = pf - _sublane_ref(pf_row, 2 * h, h - 1, low_half)
    d_b = cb - _sublane_ref(cb_row, 2 * h, h, low_half)
    lhs = q * jnp.exp2(jnp.where(late, d_f, d_b))
    rhs = k_sel * jnp.exp2(-jnp.where(late, d_b, d_f))
    return lhs, rhs


def _bwd_scan_kernel(kb_ref, cb_ref, v_ref, hist_ref, st_ref):
    heads = kb_ref.shape[0]
    nchunk = kb_ref.shape[1] // SCAN_CHUNK

    @pl.when(pl.program_id(1) == 0)
    def _():
        st_ref[...] = jnp.zeros_like(st_ref)

    def body(ci, carry):
        c = nchunk - 1 - ci
        rows = pl.ds(pl.multiple_of(c * SCAN_CHUNK, SCAN_CHUNK), SCAN_CHUNK)
        for h in range(heads):
            cb = cb_ref[h, rows, :]
            st = st_ref[h]
            hist_ref[h, c] = st.astype(hist_ref.dtype)
            tot = cb[0:1, :]
            kt = (kb_ref[h, rows, :].astype(F32) * jnp.exp2(tot - cb)).astype(BF16)
            st_ref[h] = st * jnp.exp2(tot) + _dot_tn(v_ref[h, rows, :], kt)
        return carry

    lax.fori_loop(0, nchunk, body, 0, unroll=8)


def _bwd_scan_call(kb, cb, v, *, tb):
    batch, heads, seq, _ = kb.shape
    nblk = seq // tb
    spec = pl.BlockSpec((None, heads, tb, HEAD_DIM), lambda b, i: (b, 0, nblk - 1 - i, 0))
    return pl.pallas_call(
        _bwd_scan_kernel,
        grid=(batch, nblk),
        in_specs=[spec, spec, spec],
        out_specs=pl.BlockSpec((None, heads, tb // SCAN_CHUNK, HEAD_DIM, HEAD_DIM),
                               lambda b, i: (b, 0, nblk - 1 - i, 0, 0)),
        out_shape=jax.ShapeDtypeStruct(
            (batch, heads, seq // SCAN_CHUNK, HEAD_DIM, HEAD_DIM), BF16),
        scratch_shapes=[pltpu.VMEM((heads, HEAD_DIM, HEAD_DIM), F32)],
        compiler_params=pltpu.CompilerParams(
            dimension_semantics=("arbitrary", "arbitrary"), vmem_limit_bytes=VMEM_LIMIT),
        name="hgrn_bwd_scan",
    )(kb, cb, v)


def _fwd_kernel(q_ref, kf_ref, kb_ref, pf_ref, cb_ref, v_ref, g_ref, hist_ref, og_ref,
                a_ref, st_ref, score_ref, part_ref):
    heads = q_ref.shape[0]
    nchunk = q_ref.shape[1] // SCAN_CHUNK

    @pl.when(pl.program_id(2) == 0)
    def _():
        st_ref[...] = jnp.zeros_like(st_ref)

    def chunk_rows(c):
        return pl.ds(pl.multiple_of(c * SCAN_CHUNK, SCAN_CHUNK), SCAN_CHUNK)

    zeros = jnp.zeros((SCAN_CHUNK, HEAD_DIM), BF16)

    def side_by_side(even, odd):
        return jnp.concatenate([even, odd], axis=1)

    def block_diag(even, odd):
        return jnp.concatenate([side_by_side(even, zeros), side_by_side(zeros, odd)], axis=0)

    def stage1(p, c, masks):
        rows, slot = chunk_rows(c), c % 2
        late_rows, low_half, _ = masks
        operands = []
        for h in (2 * p, 2 * p + 1):
            q = q_ref[h, rows, :].astype(F32)
            kf = kf_ref[h, rows, :].astype(F32)
            kb = kb_ref[h, rows, :].astype(F32)
            pf = pf_ref[h, rows, :]
            cb = cb_ref[h, rows, :]
            v = v_ref[h, rows, :]
            pf_row = lambda r, h=h: pf_ref[h, pl.ds(c * SCAN_CHUNK + r, 1), :]
            cb_row = lambda r, h=h: cb_ref[h, pl.ds(c * SCAN_CHUNK + r, 1), :]
            tiles = [_level_operands(q, kf, kb, pf, cb, pf_row, cb_row, lvl, late_rows, low_half)
                     for lvl in LEVELS]
            tiles.append((q, kf + kb))
            operands.append([(lhs.astype(BF16), rhs.astype(BF16)) for lhs, rhs in tiles])

            st = st_ref[h]
            q_in = jnp.concatenate([q * jnp.exp2(pf), q * jnp.exp2(cb)], axis=1)
            s_in = jnp.concatenate([st.astype(BF16), hist_ref[h, c]], axis=1)
            part_ref[slot, h] = _dot_nt(q_in.astype(BF16), s_in)

            tot = pf[SCAN_CHUNK - 1:SCAN_CHUNK, :]
            kt = (kf * jnp.exp2(tot - pf)).astype(BF16)
            st_ref[h] = st * jnp.exp2(tot) + _dot_tn(v, kt)
        for li, ((lhs_e, rhs_e), (lhs_o, rhs_o)) in enumerate(zip(*operands)):
            score_ref[slot, p, li] = _dot_nt(side_by_side(lhs_e, lhs_o), block_diag(rhs_e, rhs_o))

    def stage2(p, c, masks):
        rows, slot = chunk_rows(c), c % 2
        pair_level = masks[2]
        a = score_ref[slot, p, len(LEVELS)]
        for li, lvl in enumerate(LEVELS):
            a = jnp.where(pair_level[lvl], score_ref[slot, p, li], a)
        v_pair = block_diag(v_ref[2 * p, rows, :], v_ref[2 * p + 1, rows, :])
        o_pair = _dot(a.astype(BF16), v_pair)
        for j, h in enumerate((2 * p, 2 * p + 1)):
            o = o_pair[:, j * HEAD_DIM:(j + 1) * HEAD_DIM] + part_ref[slot, h]
            o = _rms(o, og_ref[...]) * g_ref[h, rows, :].astype(F32)
            a_ref[h, rows, :] = o.astype(a_ref.dtype)

    pairs = heads // 2
    masks0 = _chunk_masks()
    for p in range(pairs):
        stage1(p, 0, masks0)

    def body(c, carry):
        masks = _chunk_masks()
        for p in range(pairs):
            stage2(p, c - 1, masks)
        for p in range(pairs):
            stage1(p, c, masks)
        return carry

    lax.fori_loop(1, nchunk, body, 0, unroll=5)
    for p in range(pairs):
        stage2(p, nchunk - 1, masks0)


def _fwd_call(q, kf, kb, pf, cb, v, g, hist, og, *, tb, hp):
    batch, heads, seq, _ = q.shape
    spec = pl.BlockSpec((None, hp, tb, HEAD_DIM), lambda b, hg, i: (b, hg, i, 0))
    return pl.pallas_call(
        _fwd_kernel,
        grid=(batch, heads // hp, seq // tb),
        in_specs=[spec] * 7 + [
            pl.BlockSpec((None, hp, tb // SCAN_CHUNK, HEAD_DIM, HEAD_DIM),
                         lambda b, hg, i: (b, hg, i, 0, 0)),
            _resident(og.shape)],
        out_specs=spec,
        out_shape=jax.ShapeDtypeStruct((batch, heads, seq, HEAD_DIM), BF16),
        scratch_shapes=[pltpu.VMEM((hp, HEAD_DIM, HEAD_DIM), F32),
                        pltpu.VMEM((2, hp // 2, len(LEVELS) + 1, SCAN_CHUNK, 2 * SCAN_CHUNK), F32),
                        pltpu.VMEM((2, hp, SCAN_CHUNK, HEAD_DIM), F32)],
        compiler_params=pltpu.CompilerParams(
            dimension_semantics=("arbitrary", "arbitrary", "arbitrary"),
            vmem_limit_bytes=VMEM_LIMIT),
        name="hgrn_fwd",
    )(q, kf, kb, pf, cb, v, g, hist, og)


def _outproj_kernel(x_ref, a_ref, s_ref, wout_ref, gffn_ref, h_ref, hn_ref):
    mix = jnp.concatenate([a_ref[h] for h in range(HEADS)] + [s_ref[...]], axis=1)
    h = x_ref[...] + _dot(mix, wout_ref[...])
    h_ref[...] = h
    hn_ref[...] = _rms(h, gffn_ref[...]).astype(hn_ref.dtype)


def _outproj_call(x2, a, s, wout, gffn, *, seq, tm):
    tokens, d_model = x2.shape
    nlb = seq // tm
    tile = pl.BlockSpec((tm, d_model), lambda i: (i, 0))
    return pl.pallas_call(
        _outproj_kernel,
        grid=(tokens // tm,),
        in_specs=[tile,
                  pl.BlockSpec((None, HEADS, tm, HEAD_DIM), lambda i: (i // nlb, 0, i % nlb, 0)),
                  pl.BlockSpec((tm, s.shape[1]), lambda i: (i, 0)),
                  _resident(wout.shape), _resident(gffn.shape)],
        out_specs=[tile, tile],
        out_shape=[jax.ShapeDtypeStruct((tokens, d_model), F32),
                   jax.ShapeDtypeStruct((tokens, d_model), BF16)],
        compiler_params=pltpu.CompilerParams(
            dimension_semantics=("arbitrary",), vmem_limit_bytes=VMEM_LIMIT),
        name="outproj",
    )(x2, a, s, wout, gffn)


def _ffn_kernel(hn_ref, wg_ref, wu_ref, wd_ref, o_ref):
    @pl.when(pl.program_id(1) == 0)
    def _():
        o_ref[...] = jnp.zeros_like(o_ref)

    hn = hn_ref[...]
    gate = _dot(hn, wg_ref[...])
    act = (gate * _sigmoid(gate) * _dot(hn, wu_ref[...])).astype(BF16)
    o_ref[...] += _dot(act, wd_ref[...])


def _ffn_call(hn, wg, wu, wd, *, tm, th):
    tokens, d_model = hn.shape
    hidden = wg.shape[1]
    tile = pl.BlockSpec((tm, d_model), lambda i, j: (i, 0))
    return pl.pallas_call(
        _ffn_kernel,
        grid=(tokens // tm, hidden // th),
        in_specs=[tile,
                  pl.BlockSpec((d_model, th), lambda i, j: (0, j)),
                  pl.BlockSpec((d_model, th), lambda i, j: (0, j)),
                  pl.BlockSpec((th, d_model), lambda i, j: (j, 0))],
        out_specs=tile,
        out_shape=jax.ShapeDtypeStruct((tokens, d_model), F32),
        compiler_params=pltpu.CompilerParams(
            dimension_semantics=("arbitrary", "arbitrary"), vmem_limit_bytes=VMEM_LIMIT),
        name="ffn",
    )(hn, wg, wu, wd)


def _ple_kernel(h_ref, f_ref, p_ref, gple_ref, wpg_ref, wpp_ref, gout_ref, o_ref):
    h2 = h_ref[...] + f_ref[...]
    hp = _rms(h2, gple_ref[...]).astype(BF16)
    pgate = _sigmoid(_dot(hp, wpg_ref[...]))
    pproj = _dot(p_ref[...].astype(BF16), wpp_ref[...])
    o_ref[...] = _rms(h2 + pgate * pproj, gout_ref[...])


def _ple_call(h, f, p2, gple, wpg, wpp, gout, *, tm):
    tokens, d_model = h.shape
    tile = pl.BlockSpec((tm, d_model), lambda i: (i, 0))
    return pl.pallas_call(
        _ple_kernel,
        grid=(tokens // tm,),
        in_specs=[tile, tile, pl.BlockSpec((tm, p2.shape[1]), lambda i: (i, 0)),
                  _resident(gple.shape), _resident(wpg.shape), _resident(wpp.shape),
                  _resident(gout.shape)],
        out_specs=tile,
        out_shape=jax.ShapeDtypeStruct((tokens, d_model), F32),
        compiler_params=pltpu.CompilerParams(
            dimension_semantics=("arbitrary",), vmem_limit_bytes=VMEM_LIMIT),
        name="ple_final",
    )(h, f, p2, gple, wpg, wpp, gout)


def _layer(h2, p2, lb_f, lb_b, norm_mix_g, w_in, hgrn_onorm_g, sgu_ln_g, sgu_ln_b, sgu_w,
           sgu_b, sgu_onorm_g, w_out, norm_ffn_g, w_gate, w_up, w_down, norm_ple_g,
           w_ple_gate, w_ple_proj, out_g, *, batch, seq):
    row = lambda a: a.reshape(1, -1).astype(F32)
    q, kf, kb, pf, cb, v, g, s = _inproj_call(
        h2, row(norm_mix_g), w_in.astype(BF16), row(lb_f), row(lb_b), row(sgu_ln_g),
        row(sgu_ln_b), sgu_w.astype(BF16), sgu_b.astype(F32)[:, :, None], row(sgu_onorm_g),
        batch=batch, seq=seq, tm=256)
    hist = _bwd_scan_call(kb, cb, v, tb=1024)
    a = _fwd_call(q, kf, kb, pf, cb, v, g, hist, row(hgrn_onorm_g), tb=1024, hp=8)
    hmid, hn = _outproj_call(h2, a, s, w_out.astype(BF16), row(norm_ffn_g), seq=seq, tm=512)
    ffn = _ffn_call(hn, w_gate.astype(BF16), w_up.astype(BF16), w_down.astype(BF16),
                    tm=1024, th=512)
    return _ple_call(hmid, ffn, p2, row(norm_ple_g), w_ple_gate.astype(BF16),
                     w_ple_proj.astype(BF16), row(out_g), tm=512)


def kernel(x, p, norm_mix_g, w_in, lb_fwd_logits, lb_bwd_logits, hgrn_onorm_g, sgu_ln_g, sgu_ln_b, sgu_w, sgu_b, sgu_onorm_g, w_out, norm_ffn_g, w_gate, w_up, w_down, norm_ple_g, w_ple_gate, w_ple_proj, final_norm_g):
    batch, seq, d_model = x.shape
    depth = w_in.shape[0]
    lb_f_all = jnp.cumsum(jax.nn.softmax(lb_fwd_logits.astype(F32), axis=0), axis=0)
    lb_b_all = jnp.cumsum(jax.nn.softmax(lb_bwd_logits.astype(F32), axis=0), axis=0)
    h = x.reshape(batch * seq, d_model)
    for layer in range(depth):
        assert layer == depth - 1
        h = _layer(h, p[layer].reshape(batch * seq, -1), lb_f_all[layer], lb_b_all[layer],
                   norm_mix_g[layer], w_in[layer], hgrn_onorm_g[layer], sgu_ln_g[layer],
                   sgu_ln_b[layer], sgu_w[layer], sgu_b[layer], sgu_onorm_g[layer],
                   w_out[layer], norm_ffn_g[layer], w_gate[layer], w_up[layer],
                   w_down[layer], norm_ple_g[layer], w_ple_gate[layer], w_ple_proj[layer],
                   final_norm_g, batch=batch, seq=seq)
    return h.reshape(batch, seq, d_model)
```

```python
import jax
import jax.numpy as jnp
from jax import lax
from jax.experimental import pallas as pl
from jax.experimental.pallas import tpu as pltpu

F32 = jnp.float32
BF16 = jnp.bfloat16
EPS = 1e-6

HEADS = 8
HEAD_DIM = 128
SUBLANES = 8
SCAN_CHUNK = 64
SGU_CHUNK = 128
LEVELS = (1, 2, 4, 8, 16, 32)
VMEM_LIMIT = 60 * 1024 * 1024


def _dot(a, b):
    return jnp.dot(a, b, preferred_element_type=F32)


def _dot_nt(a, b):
    return lax.dot_general(a, b, (((1,), (1,)), ((), ())), preferred_element_type=F32)


def _dot_tn(a, b):
    return lax.dot_general(a, b, (((0,), (0,)), ((), ())), preferred_element_type=F32)


def _sigmoid(x):
    return 1.0 / (1.0 + jnp.exp(-x))


def _rms(x, g):
    return x * lax.rsqrt(jnp.mean(x * x, axis=-1, keepdims=True) + EPS) * g


def _unpack(w_ref_or_val):
    return pltpu.bitcast(w_ref_or_val, BF16)


def _pack_kernel(w_ref, o_ref):
    o_ref[...] = pltpu.bitcast(w_ref[...].astype(BF16), jnp.uint32)


def _pack_weight_call(w, *, rows):
    k, n = w.shape
    return pl.pallas_call(
        _pack_kernel,
        grid=(k // rows,),
        in_specs=[pl.BlockSpec((rows, n), lambda i: (i, 0))],
        out_specs=pl.BlockSpec((rows // 2, n), lambda i: (i, 0)),
        out_shape=jax.ShapeDtypeStruct((k // 2, n), jnp.uint32),
        compiler_params=pltpu.CompilerParams(
            dimension_semantics=("arbitrary",), vmem_limit_bytes=VMEM_LIMIT),
        name="pack_weight",
    )(w)


def _resident(shape):
    nd = len(shape)
    return pl.BlockSpec(shape, lambda *_: (0,) * nd, pipeline_mode=pl.Buffered(1))


def _chunk_cumsum(x, reverse):
    rows, cols = x.shape
    x3 = x.reshape(rows // SUBLANES, SUBLANES, cols)
    sub = lax.broadcasted_iota(jnp.int32, x3.shape, 1)
    s = 1
    while s < SUBLANES:
        if reverse:
            x3 = x3 + jnp.where(sub < SUBLANES - s, pltpu.roll(x3, SUBLANES - s, axis=1), 0.0)
        else:
            x3 = x3 + jnp.where(sub >= s, pltpu.roll(x3, s, axis=1), 0.0)
        s *= 2
    groups = SCAN_CHUNK // SUBLANES
    edge = 0 if reverse else SUBLANES - 1
    out = [None] * (rows // SUBLANES)
    for c in range(rows // SCAN_CHUNK):
        carry = None
        order = range(groups - 1, -1, -1) if reverse else range(groups)
        for j in order:
            blk = x3[c * groups + j]
            total = blk[edge:edge + 1, :]
            out[c * groups + j] = blk if carry is None else blk + carry
            carry = total if carry is None else carry + total
    return jnp.concatenate(out, axis=0)


def _sgu(u_raw, v_raw, lng, lnb, sw_ref, sb_ref, sog, s_ref):
    def gelu(t):
        return 0.5 * t * (1.0 + lax.erf(t * (2.0 ** -0.5)))

    v = gelu(v_raw)
    vc = v - jnp.mean(v, axis=-1, keepdims=True)
    v = vc * lax.rsqrt(jnp.mean(vc * vc, axis=-1, keepdims=True) + EPS)
    v = (v * lng + lnb).astype(BF16)
    u = gelu(u_raw)
    for c in range(u.shape[0] // SGU_CHUNK):
        rows = slice(c * SGU_CHUNK, (c + 1) * SGU_CHUNK)
        mixed = [
            _dot(sw_ref[gi], v[rows, gi * HEAD_DIM:(gi + 1) * HEAD_DIM]) + sb_ref[gi]
            for gi in range(HEADS)
        ]
        s_ref[rows, :] = _rms(u[rows] * jnp.concatenate(mixed, axis=1), sog).astype(s_ref.dtype)


def _inproj_kernel(x_ref, gmix_ref, win_ref, lbf_ref, lbb_ref, lng_ref, lnb_ref,
                   sw_ref, sb_ref, sog_ref,
                   q_ref, kf_ref, kb_ref, pf_ref, cb_ref, v_ref, g_ref, s_ref):
    width = HEADS * HEAD_DIM
    xb = _rms(x_ref[...], gmix_ref[...]).astype(BF16)

    def proj(j):
        return _dot(xb, _unpack(win_ref[:, j * width:(j + 1) * width]))

    def put_heads(ref, val):
        for h in range(HEADS):
            ref[h] = val[:, h * HEAD_DIM:(h + 1) * HEAD_DIM].astype(ref.dtype)

    v_raw = proj(6)
    _sgu(proj(5), v_raw, lng_ref[...], lnb_ref[...], sw_ref, sb_ref, sog_ref[...], s_ref)

    def gates(j, lb_ref, k_ref, cum_ref, reverse):
        half = width // 2
        for part in range(2):
            cols = slice(part * half, (part + 1) * half)
            lb = lb_ref[:, cols]
            z = _dot(xb, _unpack(
                win_ref[:, j * width + part * half:j * width + (part + 1) * half]))
            sig = _sigmoid(z)
            cum = _chunk_cumsum(jnp.log2(lb + (1.0 - lb) * sig), reverse)
            k = (1.0 - lb) * (1.0 - sig)
            for h in range(HEADS // 2):
                hh = part * (HEADS // 2) + h
                cum_ref[hh] = cum[:, h * HEAD_DIM:(h + 1) * HEAD_DIM]
                k_ref[hh] = k[:, h * HEAD_DIM:(h + 1) * HEAD_DIM].astype(k_ref.dtype)

    gates(1, lbf_ref, kf_ref, pf_ref, False)
    q = proj(0)
    put_heads(q_ref, q * _sigmoid(q))
    gates(2, lbb_ref, kb_ref, cb_ref, True)
    g = proj(4)
    put_heads(g_ref, g * _sigmoid(g))
    put_heads(v_ref, proj(3))


def _inproj_call(x2, gmix, win, lbf, lbb, lng, lnb, sw, sb, sog, *, batch, seq, tm):
    tokens, d_model = x2.shape
    width = HEADS * HEAD_DIM
    nlb = seq // tm
    row = lambda a: _resident(a.shape)
    head_spec = pl.BlockSpec((None, HEADS, tm, HEAD_DIM), lambda i: (i // nlb, 0, i % nlb, 0))
    head_shape = lambda dt: jax.ShapeDtypeStruct((batch, HEADS, seq, HEAD_DIM), dt)
    return pl.pallas_call(
        _inproj_kernel,
        grid=(tokens // tm,),
        in_specs=[pl.BlockSpec((tm, d_model), lambda i: (i, 0)),
                  row(gmix), row(win), row(lbf), row(lbb), row(lng), row(lnb),
                  row(sw), row(sb), row(sog)],
        out_specs=[head_spec] * 7 + [pl.BlockSpec((tm, width), lambda i: (i, 0))],
        out_shape=[head_shape(BF16), head_shape(BF16), head_shape(BF16),
                   head_shape(F32), head_shape(F32), head_shape(BF16), head_shape(BF16),
                   jax.ShapeDtypeStruct((tokens, width), BF16)],
        compiler_params=pltpu.CompilerParams(
            dimension_semantics=("arbitrary",), vmem_limit_bytes=VMEM_LIMIT),
        name="inproj",
    )(x2, gmix, win, lbf, lbb, lng, lnb, sw, sb, sog)


def _chunk_masks():
    sub = lax.broadcasted_iota(jnp.int32, (SCAN_CHUNK, HEAD_DIM), 0) & (SUBLANES - 1)
    pair_xor = (lax.broadcasted_iota(jnp.int32, (SCAN_CHUNK, 2 * SCAN_CHUNK), 0)
                ^ (lax.broadcasted_iota(jnp.int32, (SCAN_CHUNK, 2 * SCAN_CHUNK), 1)
                   & (SCAN_CHUNK - 1)))
    late = {h: (sub & h) != 0 for h in LEVELS if h < SUBLANES}
    return late, sub < SUBLANES // 2, {h: pair_xor >= h for h in LEVELS}


def _sublane_ref(row, rows_per_group, r0, low_half):
    def pick(r):
        return jnp.concatenate(
            [jnp.broadcast_to(row(j * SUBLANES + r), (SUBLANES, HEAD_DIM))
             for j in range(SCAN_CHUNK // SUBLANES)], axis=0)
    if rows_per_group == SUBLANES:
        return pick(r0)
    return jnp.where(low_half, pick(r0), pick(r0 + rows_per_group))


def _level_operands(q, kf, kb, pf, cb, pf_row, cb_row, h, late_rows, low_half):
    if h >= SUBLANES:
        lhs, rhs = [], []
        for r0 in range(0, SCAN_CHUNK, 2 * h):
            early, late = slice(r0, r0 + h), slice(r0 + h, r0 + 2 * h)
            ref_f = pf_row(r0 + h - 1)
            ref_b = cb_row(r0 + h)
            lhs += [q[early] * jnp.exp2(cb[early] - ref_b), q[late] * jnp.exp2(pf[late] - ref_f)]
            rhs += [kf[early] * jnp.exp2(ref_f - pf[early]), kb[late] * jnp.exp2(ref_b - cb[late])]
        return jnp.concatenate(lhs, axis=0), jnp.concatenate(rhs, axis=0)
    late = late_rows[h]
    k_sel = jnp.where(late, kb, kf)
    if h == 1:
        d_f = pf - pltpu.roll(pf, 1, axis=0)
        d_b = cb - pltpu.roll(cb, SCAN_CHUNK - 1, axis=0)
        return q * jnp.exp2(jnp.where(late, d_f, d_b)), k_sel
    d_f = pf - _sublane_ref(pf_row, 2 * h, h - 1, low_half)
    d_b = cb - _sublane_ref(cb_row, 2 * h, h, low_half)
    lhs = q * jnp.exp2(jnp.where(late, d_f, d_b))
    rhs = k_sel * jnp.exp2(-jnp.where(late, d_b, d_f))
    return lhs, rhs


def _bwd_scan_kernel(kb_ref, cb_ref, v_ref, hist_ref, st_ref):
    heads = kb_ref.shape[0]
    nchunk = kb_ref.shape[1] // SCAN_CHUNK

    @pl.when(pl.program_id(1) == 0)
    def _():
        st_ref[...] = jnp.zeros_like(st_ref)

    def body(ci, carry):
        c = nchunk - 1 - ci
        rows = pl.ds(pl.multiple_of(c * SCAN_CHUNK, SCAN_CHUNK), SCAN_CHUNK)
        for h in range(heads):
            cb = cb_ref[h, rows, :]
            st = st_ref[h]
            hist_ref[h, c] = st.astype(hist_ref.dtype)
            tot = cb[0:1, :]
            kt = (kb_ref[h, rows, :].astype(F32) * jnp.exp2(tot - cb)).astype(BF16)
            st_ref[h] = st * jnp.exp2(tot) + _dot_tn(v_ref[h, rows, :], kt)
        return carry

    lax.fori_loop(0, nchunk, body, 0, unroll=8)


def _bwd_scan_call(kb, cb, v, *, tb):
    batch, heads, seq, _ = kb.shape
    nblk = seq // tb
    spec = pl.BlockSpec((None, heads, tb, HEAD_DIM), lambda b, i: (b, 0, nblk - 1 - i, 0))
    return pl.pallas_call(
        _bwd_scan_kernel,
        grid=(batch, nblk),
        in_specs=[spec, spec, spec],
        out_specs=pl.BlockSpec((None, heads, tb // SCAN_CHUNK, HEAD_DIM, HEAD_DIM),
                               lambda b, i: (b, 0, nblk - 1 - i, 0, 0)),
        out_shape=jax.ShapeDtypeStruct(
            (batch, heads, seq // SCAN_CHUNK, HEAD_DIM, HEAD_DIM), BF16),
        scratch_shapes=[pltpu.VMEM((heads, HEAD_DIM, HEAD_DIM), F32)],
        compiler_params=pltpu.CompilerParams(
            dimension_semantics=("arbitrary", "arbitrary"), vmem_limit_bytes=VMEM_LIMIT),
        name="hgrn_bwd_scan",
    )(kb, cb, v)


def _fwd_kernel(q_ref, kf_ref, kb_ref, pf_ref, cb_ref, v_ref, g_ref, hist_ref, og_ref,
                a_ref, st_ref, score_ref, part_ref):
    heads = q_ref.shape[0]
    nchunk = q_ref.shape[1] // SCAN_CHUNK

    @pl.when(pl.program_id(2) == 0)
    def _():
        st_ref[...] = jnp.zeros_like(st_ref)

    def chunk_rows(c):
        return pl.ds(pl.multiple_of(c * SCAN_CHUNK, SCAN_CHUNK), SCAN_CHUNK)

    zeros = jnp.zeros((SCAN_CHUNK, HEAD_DIM), BF16)

    def side_by_side(even, odd):
        return jnp.concatenate([even, odd], axis=1)

    def block_diag(even, odd):
        return jnp.concatenate([side_by_side(even, zeros), side_by_side(zeros, odd)], axis=0)

    def stage1(p, c, masks):
        rows, slot = chunk_rows(c), c % 2
        late_rows, low_half, _ = masks
        operands = []
        for h in (2 * p, 2 * p + 1):
            q = q_ref[h, rows, :].astype(F32)
            kf = kf_ref[h, rows, :].astype(F32)
            kb = kb_ref[h, rows, :].astype(F32)
            pf = pf_ref[h, rows, :]
            cb = cb_ref[h, rows, :]
            v = v_ref[h, rows, :]
            pf_row = lambda r, h=h: pf_ref[h, pl.ds(c * SCAN_CHUNK + r, 1), :]
            cb_row = lambda r, h=h: cb_ref[h, pl.ds(c * SCAN_CHUNK + r, 1), :]
            tiles = [_level_operands(q, kf, kb, pf, cb, pf_row, cb_row, lvl, late_rows, low_half)
                     for lvl in LEVELS]
            tiles.append((q, kf + kb))
            operands.append([(lhs.astype(BF16), rhs.astype(BF16)) for lhs, rhs in tiles])

            st = st_ref[h]
            q_in = jnp.concatenate([q * jnp.exp2(pf), q * jnp.exp2(cb)], axis=1)
            s_in = jnp.concatenate([st.astype(BF16), hist_ref[h, c]], axis=1)
            part_ref[slot, h] = _dot_nt(q_in.astype(BF16), s_in)

            tot = pf[SCAN_CHUNK - 1:SCAN_CHUNK, :]
            kt = (kf * jnp.exp2(tot - pf)).astype(BF16)
            st_ref[h] = st * jnp.exp2(tot) + _dot_tn(v, kt)
        for li, ((lhs_e, rhs_e), (lhs_o, rhs_o)) in enumerate(zip(*operands)):
            score_ref[slot, p, li] = _dot_nt(side_by_side(lhs_e, lhs_o), block_diag(rhs_e, rhs_o))

    def stage2(p, c, masks):
        rows, slot = chunk_rows(c), c % 2
        pair_level = masks[2]
        a = score_ref[slot, p, len(LEVELS)]
        for li, lvl in enumerate(LEVELS):
            a = jnp.where(pair_level[lvl], score_ref[slot, p, li], a)
        v_pair = block_diag(v_ref[2 * p, rows, :], v_ref[2 * p + 1, rows, :])
        o_pair = _dot(a.astype(BF16), v_pair)
        for j, h in enumerate((2 * p, 2 * p + 1)):
            o = o_pair[:, j * HEAD_DIM:(j + 1) * HEAD_DIM] + part_ref[slot, h]
            o = _rms(o, og_ref[...]) * g_ref[h, rows, :].astype(F32)
            a_ref[h, rows, :] = o.astype(a_ref.dtype)

    pairs = heads // 2
    masks0 = _chunk_masks()
    for p in range(pairs):
        stage1(p, 0, masks0)

    def body(c, carry):
        masks = _chunk_masks()
        for p in range(pairs):
            stage2(p, c - 1, masks)
        for p in range(pairs):
            stage1(p, c, masks)
        return carry

    lax.fori_loop(1, nchunk, body, 0, unroll=5)
    for p in range(pairs):
        stage2(p, nchunk - 1, masks0)


def _fwd_call(q, kf, kb, pf, cb, v, g, hist, og, *, tb, hp):
    batch, heads, seq, _ = q.shape
    spec = pl.BlockSpec((None, hp, tb, HEAD_DIM), lambda b, hg, i: (b, hg, i, 0))
    return pl.pallas_call(
        _fwd_kernel,
        grid=(batch, heads // hp, seq // tb),
        in_specs=[spec] * 7 + [
            pl.BlockSpec((None, hp, tb // SCAN_CHUNK, HEAD_DIM, HEAD_DIM),
                         lambda b, hg, i: (b, hg, i, 0, 0)),
            _resident(og.shape)],
        out_specs=spec,
        out_shape=jax.ShapeDtypeStruct((batch, heads, seq, HEAD_DIM), BF16),
        scratch_shapes=[pltpu.VMEM((hp, HEAD_DIM, HEAD_DIM), F32),
                        pltpu.VMEM((2, hp // 2, len(LEVELS) + 1, SCAN_CHUNK, 2 * SCAN_CHUNK), F32),
                        pltpu.VMEM((2, hp, SCAN_CHUNK, HEAD_DIM), F32)],
        compiler_params=pltpu.CompilerParams(
            dimension_semantics=("arbitrary", "arbitrary", "arbitrary"),
            vmem_limit_bytes=VMEM_LIMIT),
        name="hgrn_fwd",
    )(q, kf, kb, pf, cb, v, g, hist, og)


def _outproj_kernel(x_ref, a_ref, s_ref, wout_ref, gffn_ref, h_ref, hn_ref):
    mix = jnp.concatenate([a_ref[h] for h in range(HEADS)] + [s_ref[...]], axis=1)
    h = x_ref[...] + _dot(mix, _unpack(wout_ref[...]))
    h_ref[...] = h
    hn_ref[...] = _rms(h, gffn_ref[...]).astype(hn_ref.dtype)


def _outproj_call(x2, a, s, wout, gffn, *, seq, tm):
    tokens, d_model = x2.shape
    nlb = seq // tm
    tile = pl.BlockSpec((tm, d_model), lambda i: (i, 0))
    return pl.pallas_call(
        _outproj_kernel,
        grid=(tokens // tm,),
        in_specs=[tile,
                  pl.BlockSpec((None, HEADS, tm, HEAD_DIM), lambda i: (i // nlb, 0, i % nlb, 0)),
                  pl.BlockSpec((tm, s.shape[1]), lambda i: (i, 0)),
                  _resident(wout.shape), _resident(gffn.shape)],
        out_specs=[tile, tile],
        out_shape=[jax.ShapeDtypeStruct((tokens, d_model), F32),
                   jax.ShapeDtypeStruct((tokens, d_model), BF16)],
        compiler_params=pltpu.CompilerParams(
            dimension_semantics=("arbitrary",), vmem_limit_bytes=VMEM_LIMIT),
        name="outproj",
    )(x2, a, s, wout, gffn)


def _ffn_kernel(hn_ref, wg_ref, wu_ref, wd_ref, o_ref):
    @pl.when(pl.program_id(1) == 0)
    def _():
        o_ref[...] = jnp.zeros_like(o_ref)

    hn = hn_ref[...]
    gate = _dot(hn, wg_ref[...])
    act = (gate * _sigmoid(gate) * _dot(hn, wu_ref[...])).astype(BF16)
    o_ref[...] += _dot(act, wd_ref[...])


def _ffn_call(hn, wg, wu, wd, *, tm, th):
    tokens, d_model = hn.shape
    hidden = wg.shape[1]
    tile = pl.BlockSpec((tm, d_model), lambda i, j: (i, 0))
    return pl.pallas_call(
        _ffn_kernel,
        grid=(tokens // tm, hidden // th),
        in_specs=[tile,
                  pl.BlockSpec((d_model, th), lambda i, j: (0, j)),
                  pl.BlockSpec((d_model, th), lambda i, j: (0, j)),
                  pl.BlockSpec((th, d_model), lambda i, j: (j, 0))],
        out_specs=tile,
        out_shape=jax.ShapeDtypeStruct((tokens, d_model), F32),
        compiler_params=pltpu.CompilerParams(
            dimension_semantics=("arbitrary", "arbitrary"), vmem_limit_bytes=VMEM_LIMIT),
        name="ffn",
    )(hn, wg, wu, wd)


def _ple_kernel(h_ref, f_ref, p_ref, gple_ref, wpg_ref, wpp_ref, gout_ref, o_ref):
    h2 = h_ref[...] + f_ref[...]
    hp = _rms(h2, gple_ref[...]).astype(BF16)
    pgate = _sigmoid(_dot(hp, _unpack(wpg_ref[...])))
    pproj = _dot(p_ref[...].astype(BF16), wpp_ref[...])
    o_ref[...] = _rms(h2 + pgate * pproj, gout_ref[...])


def _ple_call(h, f, p2, gple, wpg, wpp, gout, *, tm):
    tokens, d_model = h.shape
    tile = pl.BlockSpec((tm, d_model), lambda i: (i, 0))
    return pl.pallas_call(
        _ple_kernel,
        grid=(tokens // tm,),
        in_specs=[tile, tile, pl.BlockSpec((tm, p2.shape[1]), lambda i: (i, 0)),
                  _resident(gple.shape), _resident(wpg.shape), _resident(wpp.shape),
                  _resident(gout.shape)],
        out_specs=tile,
        out_shape=jax.ShapeDtypeStruct((tokens, d_model), F32),
        compiler_params=pltpu.CompilerParams(
            dimension_semantics=("arbitrary",), vmem_limit_bytes=VMEM_LIMIT),
        name="ple_final",
    )(h, f, p2, gple, wpg, wpp, gout)


def _layer(h2, p2, lb_f, lb_b, norm_mix_g, w_in, hgrn_onorm_g, sgu_ln_g, sgu_ln_b, sgu_w,
           sgu_b, sgu_onorm_g, w_out, norm_ffn_g, w_gate, w_up, w_down, norm_ple_g,
           w_ple_gate, w_ple_proj, out_g, *, batch, seq):
    row = lambda a: a.reshape(1, -1).astype(F32)
    q, kf, kb, pf, cb, v, g, s = _inproj_call(
        h2, row(norm_mix_g), _pack_weight_call(w_in, rows=256), row(lb_f), row(lb_b),
        row(sgu_ln_g),
        row(sgu_ln_b), sgu_w.astype(BF16), sgu_b.astype(F32)[:, :, None], row(sgu_onorm_g),
        batch=batch, seq=seq, tm=256)
    hist = _bwd_scan_call(kb, cb, v, tb=1024)
    a = _fwd_call(q, kf, kb, pf, cb, v, g, hist, row(hgrn_onorm_g), tb=1024, hp=8)
    hmid, hn = _outproj_call(h2, a, s, _pack_weight_call(w_out, rows=512), row(norm_ffn_g),
                             seq=seq, tm=512)
    ffn = _ffn_call(hn, w_gate.astype(BF16), w_up.astype(BF16), w_down.astype(BF16),
                    tm=1024, th=512)
    return _ple_call(hmid, ffn, p2, row(norm_ple_g), _pack_weight_call(w_ple_gate, rows=512),
                     w_ple_proj.astype(BF16), row(out_g), tm=512)


def kernel(x, p, norm_mix_g, w_in, lb_fwd_logits, lb_bwd_logits, hgrn_onorm_g, sgu_ln_g, sgu_ln_b, sgu_w, sgu_b, sgu_onorm_g, w_out, norm_ffn_g, w_gate, w_up, w_down, norm_ple_g, w_ple_gate, w_ple_proj, final_norm_g):
    batch, seq, d_model = x.shape
    depth = w_in.shape[0]
    lb_f_all = jnp.cumsum(jax.nn.softmax(lb_fwd_logits.astype(F32), axis=0), axis=0)
    lb_b_all = jnp.cumsum(jax.nn.softmax(lb_bwd_logits.astype(F32), axis=0), axis=0)
    h = x.reshape(batch * seq, d_model)
    for layer in range(depth):
        assert layer == depth - 1
        h = _layer(h, p[layer].reshape(batch * seq, -1), lb_f_all[layer], lb_b_all[layer],
                   norm_mix_g[layer], w_in[layer], hgrn_onorm_g[layer], sgu_ln_g[layer],
                   sgu_ln_b[layer], sgu_w[layer], sgu_b[layer], sgu_onorm_g[layer],
                   w_out[layer], norm_ffn_g[layer], w_gate[layer], w_up[layer],
                   w_down[layer], norm_ple_g[layer], w_ple_gate[layer], w_ple_proj[layer],
                   final_norm_g, batch=batch, seq=seq)
    return h.reshape(batch, seq, d_model)
```

```python
import jax
import jax.numpy as jnp
from jax import lax
from jax.experimental import pallas as pl
from jax.experimental.pallas import tpu as pltpu

F32 = jnp.float32
BF16 = jnp.bfloat16
EPS = 1e-6

HEADS = 8
HEAD_DIM = 128
SUBLANES = 8
SCAN_CHUNK = 64
SGU_CHUNK = 128
LEVELS = (1, 2, 4, 8, 16, 32)
GATE_COL_PARTS = 2
GATE_ROW_PARTS = 2
VMEM_LIMIT = 60 * 1024 * 1024


def _dot(a, b):
    return jnp.dot(a, b, preferred_element_type=F32)


def _dot_nt(a, b):
    return lax.dot_general(a, b, (((1,), (1,)), ((), ())), preferred_element_type=F32)


def _dot_tn(a, b):
    return lax.dot_general(a, b, (((0,), (0,)), ((), ())), preferred_element_type=F32)


def _sigmoid(x):
    return 1.0 / (1.0 + jnp.exp(-x))


def _rms(x, g):
    return x * lax.rsqrt(jnp.mean(x * x, axis=-1, keepdims=True) + EPS) * g


def _unpack(w_ref_or_val):
    return pltpu.bitcast(w_ref_or_val, BF16)


def _pack_kernel(w_ref, o_ref):
    o_ref[...] = pltpu.bitcast(w_ref[...].astype(BF16), jnp.uint32)


def _pack_weight_call(w, *, rows):
    k, n = w.shape
    return pl.pallas_call(
        _pack_kernel,
        grid=(k // rows,),
        in_specs=[pl.BlockSpec((rows, n), lambda i: (i, 0))],
        out_specs=pl.BlockSpec((rows // 2, n), lambda i: (i, 0)),
        out_shape=jax.ShapeDtypeStruct((k // 2, n), jnp.uint32),
        compiler_params=pltpu.CompilerParams(
            dimension_semantics=("arbitrary",), vmem_limit_bytes=VMEM_LIMIT),
        name="pack_weight",
    )(w)


def _resident(shape):
    nd = len(shape)
    return pl.BlockSpec(shape, lambda *_: (0,) * nd, pipeline_mode=pl.Buffered(1))


def _chunk_cumsum(x, reverse):
    rows, cols = x.shape
    x3 = x.reshape(rows // SUBLANES, SUBLANES, cols)
    sub = lax.broadcasted_iota(jnp.int32, x3.shape, 1)
    s = 1
    while s < SUBLANES:
        if reverse:
            x3 = x3 + jnp.where(sub < SUBLANES - s, pltpu.roll(x3, SUBLANES - s, axis=1), 0.0)
        else:
            x3 = x3 + jnp.where(sub >= s, pltpu.roll(x3, s, axis=1), 0.0)
        s *= 2
    groups = SCAN_CHUNK // SUBLANES
    edge = 0 if reverse else SUBLANES - 1
    out = [None] * (rows // SUBLANES)
    for c in range(rows // SCAN_CHUNK):
        carry = None
        order = range(groups - 1, -1, -1) if reverse else range(groups)
        for j in order:
            blk = x3[c * groups + j]
            total = blk[edge:edge + 1, :]
            out[c * groups + j] = blk if carry is None else blk + carry
            carry = total if carry is None else carry + total
    return jnp.concatenate(out, axis=0)


def _sgu(u_raw, v_raw, lng, lnb, sw_ref, sb_ref, sog, s_ref):
    def gelu(t):
        return 0.5 * t * (1.0 + lax.erf(t * (2.0 ** -0.5)))

    v = gelu(v_raw)
    vc = v - jnp.mean(v, axis=-1, keepdims=True)
    v = vc * lax.rsqrt(jnp.mean(vc * vc, axis=-1, keepdims=True) + EPS)
    v = (v * lng + lnb).astype(BF16)
    u = gelu(u_raw)
    for c in range(u.shape[0] // SGU_CHUNK):
        rows = slice(c * SGU_CHUNK, (c + 1) * SGU_CHUNK)
        mixed = [
            _dot(sw_ref[gi], v[rows, gi * HEAD_DIM:(gi + 1) * HEAD_DIM]) + sb_ref[gi]
            for gi in range(HEADS)
        ]
        s_ref[rows, :] = _rms(u[rows] * jnp.concatenate(mixed, axis=1), sog).astype(s_ref.dtype)


def _inproj_kernel(x_ref, gmix_ref, win_ref, lbf_ref, lbb_ref, lng_ref, lnb_ref,
                   sw_ref, sb_ref, sog_ref,
                   q_ref, kf_ref, kb_ref, pf_ref, cb_ref, v_ref, g_ref, s_ref):
    width = HEADS * HEAD_DIM
    xb = _rms(x_ref[...], gmix_ref[...]).astype(BF16)

    def proj(j):
        return _dot(xb, _unpack(win_ref[:, j * width:(j + 1) * width]))

    def put_heads(ref, val):
        for h in range(HEADS):
            ref[h] = val[:, h * HEAD_DIM:(h + 1) * HEAD_DIM].astype(ref.dtype)

    v_raw = proj(6)
    _sgu(proj(5), v_raw, lng_ref[...], lnb_ref[...], sw_ref, sb_ref, sog_ref[...], s_ref)

    def gates(j, lb_ref, k_ref, cum_ref, reverse):
        piece_w = width // GATE_COL_PARTS
        piece_h = xb.shape[0] // GATE_ROW_PARTS
        heads_per_piece = piece_w // HEAD_DIM
        for part in range(GATE_COL_PARTS):
            lb = lb_ref[:, part * piece_w:(part + 1) * piece_w]
            col = j * width + part * piece_w
            w = _unpack(win_ref[:, col:col + piece_w])
            for r0 in range(0, xb.shape[0], piece_h):
                rows = slice(r0, r0 + piece_h)
                sig = _sigmoid(_dot(xb[rows], w))
                cum = _chunk_cumsum(jnp.log2(lb + (1.0 - lb) * sig), reverse)
                k = (1.0 - lb) * (1.0 - sig)
                for h in range(heads_per_piece):
                    hh = part * heads_per_piece + h
                    cum_ref[hh, rows, :] = cum[:, h * HEAD_DIM:(h + 1) * HEAD_DIM]
                    k_ref[hh, rows, :] = k[:, h * HEAD_DIM:(h + 1) * HEAD_DIM].astype(k_ref.dtype)

    gates(1, lbf_ref, kf_ref, pf_ref, False)
    q = proj(0)
    put_heads(q_ref, q * _sigmoid(q))
    gates(2, lbb_ref, kb_ref, cb_ref, True)
    g = proj(4)
    put_heads(g_ref, g * _sigmoid(g))
    put_heads(v_ref, proj(3))


def _inproj_call(x2, gmix, win, lbf, lbb, lng, lnb, sw, sb, sog, *, batch, seq, tm):
    tokens, d_model = x2.shape
    width = HEADS * HEAD_DIM
    nlb = seq // tm
    row = lambda a: _resident(a.shape)
    head_spec = pl.BlockSpec((None, HEADS, tm, HEAD_DIM), lambda i: (i // nlb, 0, i % nlb, 0))
    head_shape = lambda dt: jax.ShapeDtypeStruct((batch, HEADS, seq, HEAD_DIM), dt)
    return pl.pallas_call(
        _inproj_kernel,
        grid=(tokens // tm,),
        in_specs=[pl.BlockSpec((tm, d_model), lambda i: (i, 0)),
                  row(gmix), row(win), row(lbf), row(lbb), row(lng), row(lnb),
                  row(sw), row(sb), row(sog)],
        out_specs=[head_spec] * 7 + [pl.BlockSpec((tm, width), lambda i: (i, 0))],
        out_shape=[head_shape(BF16), head_shape(BF16), head_shape(BF16),
                   head_shape(F32), head_shape(F32), head_shape(BF16), head_shape(BF16),
                   jax.ShapeDtypeStruct((tokens, width), BF16)],
        compiler_params=pltpu.CompilerParams(
            dimension_semantics=("arbitrary",), vmem_limit_bytes=VMEM_LIMIT),
        name="inproj",
    )(x2, gmix, win, lbf, lbb, lng, lnb, sw, sb, sog)


def _chunk_masks():
    sub = lax.broadcasted_iota(jnp.int32, (SCAN_CHUNK, HEAD_DIM), 0) & (SUBLANES - 1)
    pair_xor = (lax.broadcasted_iota(jnp.int32, (SCAN_CHUNK, 2 * SCAN_CHUNK), 0)
                ^ (lax.broadcasted_iota(jnp.int32, (SCAN_CHUNK, 2 * SCAN_CHUNK), 1)
                   & (SCAN_CHUNK - 1)))
    late = {h: (sub & h) != 0 for h in LEVELS if h < SUBLANES}
    return late, sub < SUBLANES // 2, {h: pair_xor >= h for h in LEVELS}


def _sublane_ref(row, rows_per_group, r0, low_half):
    def pick(r):
        return jnp.concatenate(
            [jnp.broadcast_to(row(j * SUBLANES + r), (SUBLANES, HEAD_DIM))
             for j in range(SCAN_CHUNK // SUBLANES)], axis=0)
    if rows_per_group == SUBLANES:
        return pick(r0)
    return jnp.where(low_half, pick(r0), pick(r0 + rows_per_group))


def _level_operands(q, kf, kb, pf, cb, pf_row, cb_row, h, late_rows, low_half):
    if h >= SUBLANES:
        lhs, rhs = [], []
        for r0 in range(0, SCAN_CHUNK, 2 * h):
            early, late = slice(r0, r0 + h), slice(r0 + h, r0 + 2 * h)
            ref_f = pf_row(r0 + h - 1)
            ref_b = cb_row(r0 + h)
            lhs += [q[early] * jnp.exp2(cb[early] - ref_b), q[late] * jnp.exp2(pf[late] - ref_f)]
            rhs += [kf[early] * jnp.exp2(ref_f - pf[early]), kb[late] * jnp.exp2(ref_b - cb[late])]
        return jnp.concatenate(lhs, axis=0), jnp.concatenate(rhs, axis=0)
    late = late_rows[h]
    k_sel = jnp.where(late, kb, kf)
    if h == 1:
        d_f = pf - pltpu.roll(pf, 1, axis=0)
        d_b = cb - pltpu.roll(cb, SCAN_CHUNK - 1, axis=0)
        return q * jnp.exp2(jnp.where(late, d_f, d_b)), k_sel
    d_f = pf - _sublane_ref(pf_row, 2 * h, h - 1, low_half)
    d_b = cb - _sublane_ref(cb_row, 2 * h, h, low_half)
    lhs = q * jnp.exp2(jnp.where(late, d_f, d_b))
    rhs = k_sel * jnp.exp2(-jnp.where(late, d_b, d_f))
    return lhs, rhs


def _bwd_scan_kernel(kb_ref, cb_ref, v_ref, hist_ref, st_ref):
    heads = kb_ref.shape[0]
    nchunk = kb_ref.shape[1] // SCAN_CHUNK

    @pl.when(pl.program_id(1) == 0)
    def _():
        st_ref[...] = jnp.zeros_like(st_ref)

    def body(ci, carry):
        c = nchunk - 1 - ci
        rows = pl.ds(pl.multiple_of(c * SCAN_CHUNK, SCAN_CHUNK), SCAN_CHUNK)
        for h in range(heads):
            cb = cb_ref[h, rows, :]
            st = st_ref[h]
            hist_ref[h, c] = st.astype(hist_ref.dtype)
            tot = cb[0:1, :]
            kt = (kb_ref[h, rows, :].astype(F32) * jnp.exp2(tot - cb)).astype(BF16)
            st_ref[h] = st * jnp.exp2(tot) + _dot_tn(v_ref[h, rows, :], kt)
        return carry

    lax.fori_loop(0, nchunk, body, 0, unroll=8)


def _bwd_scan_call(kb, cb, v, *, tb):
    batch, heads, seq, _ = kb.shape
    nblk = seq // tb
    spec = pl.BlockSpec((None, heads, tb, HEAD_DIM), lambda b, i: (b, 0, nblk - 1 - i, 0))
    return pl.pallas_call(
        _bwd_scan_kernel,
        grid=(batch, nblk),
        in_specs=[spec, spec, spec],
        out_specs=pl.BlockSpec((None, heads, tb // SCAN_CHUNK, HEAD_DIM, HEAD_DIM),
                               lambda b, i: (b, 0, nblk - 1 - i, 0, 0)),
        out_shape=jax.ShapeDtypeStruct(
            (batch, heads, seq // SCAN_CHUNK, HEAD_DIM, HEAD_DIM), BF16),
        scratch_shapes=[pltpu.VMEM((heads, HEAD_DIM, HEAD_DIM), F32)],
        compiler_params=pltpu.CompilerParams(
            dimension_semantics=("arbitrary", "arbitrary"), vmem_limit_bytes=VMEM_LIMIT),
        name="hgrn_bwd_scan",
    )(kb, cb, v)


def _fwd_kernel(q_ref, kf_ref, kb_ref, pf_ref, cb_ref, v_ref, g_ref, hist_ref, og_ref,
                a_ref, st_ref, score_ref, part_ref):
    heads = q_ref.shape[0]
    nchunk = q_ref.shape[1] // SCAN_CHUNK

    @pl.when(pl.program_id(2) == 0)
    def _():
        st_ref[...] = jnp.zeros_like(st_ref)

    def chunk_rows(c):
        return pl.ds(pl.multiple_of(c * SCAN_CHUNK, SCAN_CHUNK), SCAN_CHUNK)

    zeros = jnp.zeros((SCAN_CHUNK, HEAD_DIM), BF16)

    def side_by_side(even, odd):
        return jnp.concatenate([even, odd], axis=1)

    def block_diag(even, odd):
        return jnp.concatenate([side_by_side(even, zeros), side_by_side(zeros, odd)], axis=0)

    def stage1(p, c, masks):
        rows, slot = chunk_rows(c), c % 2
        late_rows, low_half, _ = masks
        operands = []
        for h in (2 * p, 2 * p + 1):
            q = q_ref[h, rows, :].astype(F32)
            kf = kf_ref[h, rows, :].astype(F32)
            kb = kb_ref[h, rows, :].astype(F32)
            pf = pf_ref[h, rows, :]
            cb = cb_ref[h, rows, :]
            v = v_ref[h, rows, :]
            pf_row = lambda r, h=h: pf_ref[h, pl.ds(c * SCAN_CHUNK + r, 1), :]
            cb_row = lambda r, h=h: cb_ref[h, pl.ds(c * SCAN_CHUNK + r, 1), :]
            tiles = [_level_operands(q, kf, kb, pf, cb, pf_row, cb_row, lvl, late_rows, low_half)
                     for lvl in LEVELS]
            tiles.append((q, kf + kb))
            operands.append([(lhs.astype(BF16), rhs.astype(BF16)) for lhs, rhs in tiles])

            st = st_ref[h]
            q_in = jnp.concatenate([q * jnp.exp2(pf), q * jnp.exp2(cb)], axis=1)
            s_in = jnp.concatenate([st.astype(BF16), hist_ref[h, c]], axis=1)
            part_ref[slot, h] = _dot_nt(q_in.astype(BF16), s_in)

            tot = pf[SCAN_CHUNK - 1:SCAN_CHUNK, :]
            kt = (kf * jnp.exp2(tot - pf)).astype(BF16)
            st_ref[h] = st * jnp.exp2(tot) + _dot_tn(v, kt)
        for li, ((lhs_e, rhs_e), (lhs_o, rhs_o)) in enumerate(zip(*operands)):
            score_ref[slot, p, li] = _dot_nt(side_by_side(lhs_e, lhs_o), block_diag(rhs_e, rhs_o))

    def stage2(p, c, masks):
        rows, slot = chunk_rows(c), c % 2
        pair_level = masks[2]
        a = score_ref[slot, p, len(LEVELS)]
        for li, lvl in enumerate(LEVELS):
            a = jnp.where(pair_level[lvl], score_ref[slot, p, li], a)
        v_pair = block_diag(v_ref[2 * p, rows, :], v_ref[2 * p + 1, rows, :])
        o_pair = _dot(a.astype(BF16), v_pair)
        for j, h in enumerate((2 * p, 2 * p + 1)):
            o = o_pair[:, j * HEAD_DIM:(j + 1) * HEAD_DIM] + part_ref[slot, h]
            o = _rms(o, og_ref[...]) * g_ref[h, rows, :].astype(F32)
            a_ref[h, rows, :] = o.astype(a_ref.dtype)

    pairs = heads // 2
    masks0 = _chunk_masks()
    for p in range(pairs):
        stage1(p, 0, masks0)

    def body(c, carry):
        masks = _chunk_masks()
        for p in range(pairs):
            stage2(p, c - 1, masks)
        for p in range(pairs):
            stage1(p, c, masks)
        return carry

    lax.fori_loop(1, nchunk, body, 0, unroll=5)
    for p in range(pairs):
        stage2(p, nchunk - 1, masks0)


def _fwd_call(q, kf, kb, pf, cb, v, g, hist, og, *, tb, hp):
    batch, heads, seq, _ = q.shape
    spec = pl.BlockSpec((None, hp, tb, HEAD_DIM), lambda b, hg, i: (b, hg, i, 0))
    return pl.pallas_call(
        _fwd_kernel,
        grid=(batch, heads // hp, seq // tb),
        in_specs=[spec] * 7 + [
            pl.BlockSpec((None, hp, tb // SCAN_CHUNK, HEAD_DIM, HEAD_DIM),
                         lambda b, hg, i: (b, hg, i, 0, 0)),
            _resident(og.shape)],
        out_specs=spec,
        out_shape=jax.ShapeDtypeStruct((batch, heads, seq, HEAD_DIM), BF16),
        scratch_shapes=[pltpu.VMEM((hp, HEAD_DIM, HEAD_DIM), F32),
                        pltpu.VMEM((2, hp // 2, len(LEVELS) + 1, SCAN_CHUNK, 2 * SCAN_CHUNK), F32),
                        pltpu.VMEM((2, hp, SCAN_CHUNK, HEAD_DIM), F32)],
        compiler_params=pltpu.CompilerParams(
            dimension_semantics=("arbitrary", "arbitrary", "arbitrary"),
            vmem_limit_bytes=VMEM_LIMIT),
        name="hgrn_fwd",
    )(q, kf, kb, pf, cb, v, g, hist, og)


def _outproj_kernel(x_ref, a_ref, s_ref, wout_ref, gffn_ref, h_ref, hn_ref):
    mix = jnp.concatenate([a_ref[h] for h in range(HEADS)] + [s_ref[...]], axis=1)
    h = x_ref[...] + _dot(mix, _unpack(wout_ref[...]))
    h_ref[...] = h
    hn_ref[...] = _rms(h, gffn_ref[...]).astype(hn_ref.dtype)


def _outproj_call(x2, a, s, wout, gffn, *, seq, tm):
    tokens, d_model = x2.shape
    nlb = seq // tm
    tile = pl.BlockSpec((tm, d_model), lambda i: (i, 0))
    return pl.pallas_call(
        _outproj_kernel,
        grid=(tokens // tm,),
        in_specs=[tile,
                  pl.BlockSpec((None, HEADS, tm, HEAD_DIM), lambda i: (i // nlb, 0, i % nlb, 0)),
                  pl.BlockSpec((tm, s.shape[1]), lambda i: (i, 0)),
                  _resident(wout.shape), _resident(gffn.shape)],
        out_specs=[tile, tile],
        out_shape=[jax.ShapeDtypeStruct((tokens, d_model), F32),
                   jax.ShapeDtypeStruct((tokens, d_model), BF16)],
        compiler_params=pltpu.CompilerParams(
            dimension_semantics=("arbitrary",), vmem_limit_bytes=VMEM_LIMIT),
        name="outproj",
    )(x2, a, s, wout, gffn)


def _ffn_kernel(hn_ref, wg_ref, wu_ref, wd_ref, o_ref):
    @pl.when(pl.program_id(1) == 0)
    def _():
        o_ref[...] = jnp.zeros_like(o_ref)

    hn = hn_ref[...]
    gate = _dot(hn, wg_ref[...])
    act = (gate * _sigmoid(gate) * _dot(hn, wu_ref[...])).astype(BF16)
    o_ref[...] += _dot(act, wd_ref[...])


def _ffn_call(hn, wg, wu, wd, *, tm, th):
    tokens, d_model = hn.shape
    hidden = wg.shape[1]
    tile = pl.BlockSpec((tm, d_model), lambda i, j: (i, 0))
    return pl.pallas_call(
        _ffn_kernel,
        grid=(tokens // tm, hidden // th),
        in_specs=[tile,
                  pl.BlockSpec((d_model, th), lambda i, j: (0, j)),
                  pl.BlockSpec((d_model, th), lambda i, j: (0, j)),
                  pl.BlockSpec((th, d_model), lambda i, j: (j, 0))],
        out_specs=tile,
        out_shape=jax.ShapeDtypeStruct((tokens, d_model), F32),
        compiler_params=pltpu.CompilerParams(
            dimension_semantics=("arbitrary", "arbitrary"), vmem_limit_bytes=VMEM_LIMIT),
        name="ffn",
    )(hn, wg, wu, wd)


def _ple_kernel(h_ref, f_ref, p_ref, gple_ref, wpg_ref, wpp_ref, gout_ref, o_ref):
    h2 = h_ref[...] + f_ref[...]
    hp = _rms(h2, gple_ref[...]).astype(BF16)
    pgate = _sigmoid(_dot(hp, _unpack(wpg_ref[...])))
    pproj = _dot(p_ref[...].astype(BF16), wpp_ref[...])
    o_ref[...] = _rms(h2 + pgate * pproj, gout_ref[...])


def _ple_call(h, f, p2, gple, wpg, wpp, gout, *, tm):
    tokens, d_model = h.shape
    tile = pl.BlockSpec((tm, d_model), lambda i: (i, 0))
    return pl.pallas_call(
        _ple_kernel,
        grid=(tokens // tm,),
        in_specs=[tile, tile, pl.BlockSpec((tm, p2.shape[1]), lambda i: (i, 0)),
                  _resident(gple.shape), _resident(wpg.shape), _resident(wpp.shape),
                  _resident(gout.shape)],
        out_specs=tile,
        out_shape=jax.ShapeDtypeStruct((tokens, d_model), F32),
        compiler_params=pltpu.CompilerParams(
            dimension_semantics=("arbitrary",), vmem_limit_bytes=VMEM_LIMIT),
        name="ple_final",
    )(h, f, p2, gple, wpg, wpp, gout)


def _layer(h2, p2, lb_f, lb_b, norm_mix_g, w_in, hgrn_onorm_g, sgu_ln_g, sgu_ln_b, sgu_w,
           sgu_b, sgu_onorm_g, w_out, norm_ffn_g, w_gate, w_up, w_down, norm_ple_g,
           w_ple_gate, w_ple_proj, out_g, *, batch, seq):
    row = lambda a: a.reshape(1, -1).astype(F32)
    q, kf, kb, pf, cb, v, g, s = _inproj_call(
        h2, row(norm_mix_g), _pack_weight_call(w_in, rows=256), row(lb_f), row(lb_b),
        row(sgu_ln_g),
        row(sgu_ln_b), sgu_w.astype(BF16), sgu_b.astype(F32)[:, :, None], row(sgu_onorm_g),
        batch=batch, seq=seq, tm=256)
    hist = _bwd_scan_call(kb, cb, v, tb=1024)
    a = _fwd_call(q, kf, kb, pf, cb, v, g, hist, row(hgrn_onorm_g), tb=1024, hp=8)
    hmid, hn = _outproj_call(h2, a, s, _pack_weight_call(w_out, rows=512), row(norm_ffn_g),
                             seq=seq, tm=512)
    ffn = _ffn_call(hn, w_gate.astype(BF16), w_up.astype(BF16), w_down.astype(BF16),
                    tm=1024, th=512)
    return _ple_call(hmid, ffn, p2, row(norm_ple_g), _pack_weight_call(w_ple_gate, rows=512),
                     w_ple_proj.astype(BF16), row(out_g), tm=512)


def kernel(x, p, norm_mix_g, w_in, lb_fwd_logits, lb_bwd_logits, hgrn_onorm_g, sgu_ln_g, sgu_ln_b, sgu_w, sgu_b, sgu_onorm_g, w_out, norm_ffn_g, w_gate, w_up, w_down, norm_ple_g, w_ple_gate, w_ple_proj, final_norm_g):
    batch, seq, d_model = x.shape
    depth = w_in.shape[0]
    lb_f_all = jnp.cumsum(jax.nn.softmax(lb_fwd_logits.astype(F32), axis=0), axis=0)
    lb_b_all = jnp.cumsum(jax.nn.softmax(lb_bwd_logits.astype(F32), axis=0), axis=0)
    h = x.reshape(batch * seq, d_model)
    for layer in range(depth):
        assert layer == depth - 1
        h = _layer(h, p[layer].reshape(batch * seq, -1), lb_f_all[layer], lb_b_all[layer],
                   norm_mix_g[layer], w_in[layer], hgrn_onorm_g[layer], sgu_ln_g[layer],
                   sgu_ln_b[layer], sgu_w[layer], sgu_b[layer], sgu_onorm_g[layer],
                   w_out[layer], norm_ffn_g[layer], w_gate[layer], w_up[layer],
                   w_down[layer], norm_ple_g[layer], w_ple_gate[layer], w_ple_proj[layer],
                   final_norm_g, batch=batch, seq=seq)
    return h.reshape(batch, seq, d_model)
```

```python
import functools

import jax
import jax.numpy as jnp
from jax import lax
from jax.experimental import pallas as pl
from jax.experimental.pallas import tpu as pltpu

F32 = jnp.float32
BF16 = jnp.bfloat16
EPS = 1e-6

HEADS = 8
HEAD_DIM = 128
SUBLANES = 8
SCAN_CHUNK = 64
SGU_CHUNK = 128
LEVELS = (1, 2, 4, 8, 16, 32)
GATE_COL_PARTS = 2
GATE_ROW_PARTS = 2
VMEM_LIMIT = 60 * 1024 * 1024


def _dot(a, b):
    return jnp.dot(a, b, preferred_element_type=F32)


def _dot_nt(a, b):
    return lax.dot_general(a, b, (((1,), (1,)), ((), ())), preferred_element_type=F32)


def _dot_tn(a, b):
    return lax.dot_general(a, b, (((0,), (0,)), ((), ())), preferred_element_type=F32)


def _sigmoid(x):
    return 1.0 / (1.0 + jnp.exp(-x))


def _rms(x, g):
    return x * lax.rsqrt(jnp.mean(x * x, axis=-1, keepdims=True) + EPS) * g


def _unpack(w_ref_or_val):
    return pltpu.bitcast(w_ref_or_val, BF16)


def _pack_kernel(w_ref, o_ref):
    o_ref[...] = pltpu.bitcast(w_ref[...].astype(BF16), jnp.uint32)


def _pack_weight_call(w, *, rows):
    k, n = w.shape
    return pl.pallas_call(
        _pack_kernel,
        grid=(k // rows,),
        in_specs=[pl.BlockSpec((rows, n), lambda i: (i, 0))],
        out_specs=pl.BlockSpec((rows // 2, n), lambda i: (i, 0)),
        out_shape=jax.ShapeDtypeStruct((k // 2, n), jnp.uint32),
        compiler_params=pltpu.CompilerParams(
            dimension_semantics=("arbitrary",), vmem_limit_bytes=VMEM_LIMIT),
        name="pack_weight",
    )(w)


def _resident(shape):
    nd = len(shape)
    return pl.BlockSpec(shape, lambda *_: (0,) * nd, pipeline_mode=pl.Buffered(1))


def _chunk_cumsum(x, reverse):
    rows, cols = x.shape
    x3 = x.reshape(rows // SUBLANES, SUBLANES, cols)
    sub = lax.broadcasted_iota(jnp.int32, x3.shape, 1)
    s = 1
    while s < SUBLANES:
        if reverse:
            x3 = x3 + jnp.where(sub < SUBLANES - s, pltpu.roll(x3, SUBLANES - s, axis=1), 0.0)
        else:
            x3 = x3 + jnp.where(sub >= s, pltpu.roll(x3, s, axis=1), 0.0)
        s *= 2
    groups = SCAN_CHUNK // SUBLANES
    edge = 0 if reverse else SUBLANES - 1
    out = [None] * (rows // SUBLANES)
    for c in range(rows // SCAN_CHUNK):
        carry = None
        order = range(groups - 1, -1, -1) if reverse else range(groups)
        for j in order:
            blk = x3[c * groups + j]
            total = blk[edge:edge + 1, :]
            out[c * groups + j] = blk if carry is None else blk + carry
            carry = total if carry is None else carry + total
    return jnp.concatenate(out, axis=0)


def _sgu(u_raw, v_raw, lng, lnb, sw_ref, sb_ref, sog, s_ref):
    def gelu(t):
        return 0.5 * t * (1.0 + lax.erf(t * (2.0 ** -0.5)))

    v = gelu(v_raw)
    vc = v - jnp.mean(v, axis=-1, keepdims=True)
    v = vc * lax.rsqrt(jnp.mean(vc * vc, axis=-1, keepdims=True) + EPS)
    v = (v * lng + lnb).astype(BF16)
    u = gelu(u_raw)
    for c in range(u.shape[0] // SGU_CHUNK):
        rows = slice(c * SGU_CHUNK, (c + 1) * SGU_CHUNK)
        mixed = [
            _dot(sw_ref[gi], v[rows, gi * HEAD_DIM:(gi + 1) * HEAD_DIM]) + sb_ref[gi]
            for gi in range(HEADS)
        ]
        s_ref[rows, :] = _rms(u[rows] * jnp.concatenate(mixed, axis=1), sog).astype(s_ref.dtype)


def _inproj_kernel(x_ref, gmix_ref, win_ref, lbf_ref, lbb_ref, lng_ref, lnb_ref,
                   sw_ref, sb_ref, sog_ref,
                   q_ref, kf_ref, kb_ref, pf_ref, cb_ref, v_ref, g_ref, s_ref):
    width = HEADS * HEAD_DIM
    xb = _rms(x_ref[...], gmix_ref[...]).astype(BF16)

    def proj(j):
        return _dot(xb, _unpack(win_ref[:, j * width:(j + 1) * width]))

    def put_heads(ref, val):
        for h in range(HEADS):
            ref[h] = val[:, h * HEAD_DIM:(h + 1) * HEAD_DIM].astype(ref.dtype)

    v_raw = proj(6)
    _sgu(proj(5), v_raw, lng_ref[...], lnb_ref[...], sw_ref, sb_ref, sog_ref[...], s_ref)

    def gates(j, lb_ref, k_ref, cum_ref, reverse):
        piece_w = width // GATE_COL_PARTS
        piece_h = xb.shape[0] // GATE_ROW_PARTS
        heads_per_piece = piece_w // HEAD_DIM
        for part in range(GATE_COL_PARTS):
            lb = lb_ref[:, part * piece_w:(part + 1) * piece_w]
            col = j * width + part * piece_w
            w = _unpack(win_ref[:, col:col + piece_w])
            for r0 in range(0, xb.shape[0], piece_h):
                rows = slice(r0, r0 + piece_h)
                sig = _sigmoid(_dot(xb[rows], w))
                cum = _chunk_cumsum(jnp.log2(lb + (1.0 - lb) * sig), reverse)
                k = (1.0 - lb) * (1.0 - sig)
                for h in range(heads_per_piece):
                    hh = part * heads_per_piece + h
                    cum_ref[hh, rows, :] = cum[:, h * HEAD_DIM:(h + 1) * HEAD_DIM]
                    k_ref[hh, rows, :] = k[:, h * HEAD_DIM:(h + 1) * HEAD_DIM].astype(k_ref.dtype)

    gates(1, lbf_ref, kf_ref, pf_ref, False)
    q = proj(0)
    put_heads(q_ref, q * _sigmoid(q))
    gates(2, lbb_ref, kb_ref, cb_ref, True)
    g = proj(4)
    put_heads(g_ref, g * _sigmoid(g))
    put_heads(v_ref, proj(3))


def _inproj_call(x2, gmix, win, lbf, lbb, lng, lnb, sw, sb, sog, *, batch, seq, tm):
    tokens, d_model = x2.shape
    width = HEADS * HEAD_DIM
    nlb = seq // tm
    row = lambda a: _resident(a.shape)
    head_spec = pl.BlockSpec((None, HEADS, tm, HEAD_DIM), lambda i: (i // nlb, 0, i % nlb, 0))
    head_shape = lambda dt: jax.ShapeDtypeStruct((batch, HEADS, seq, HEAD_DIM), dt)
    return pl.pallas_call(
        _inproj_kernel,
        grid=(tokens // tm,),
        in_specs=[pl.BlockSpec((tm, d_model), lambda i: (i, 0)),
                  row(gmix), row(win), row(lbf), row(lbb), row(lng), row(lnb),
                  row(sw), row(sb), row(sog)],
        out_specs=[head_spec] * 7 + [pl.BlockSpec((tm, width), lambda i: (i, 0))],
        out_shape=[head_shape(BF16), head_shape(BF16), head_shape(BF16),
                   head_shape(F32), head_shape(F32), head_shape(BF16), head_shape(BF16),
                   jax.ShapeDtypeStruct((tokens, width), BF16)],
        compiler_params=pltpu.CompilerParams(
            dimension_semantics=("arbitrary",), vmem_limit_bytes=VMEM_LIMIT),
        name="inproj",
    )(x2, gmix, win, lbf, lbb, lng, lnb, sw, sb, sog)


def _chunk_masks():
    sub = lax.broadcasted_iota(jnp.int32, (SCAN_CHUNK, HEAD_DIM), 0) & (SUBLANES - 1)
    pair_xor = (lax.broadcasted_iota(jnp.int32, (SCAN_CHUNK, 2 * SCAN_CHUNK), 0)
                ^ (lax.broadcasted_iota(jnp.int32, (SCAN_CHUNK, 2 * SCAN_CHUNK), 1)
                   & (SCAN_CHUNK - 1)))
    late = {h: (sub & h) != 0 for h in LEVELS if h < SUBLANES}
    return late, sub < SUBLANES // 2, {h: pair_xor >= h for h in LEVELS}


def _sublane_ref(row, rows_per_group, r0, low_half):
    def pick(r):
        return jnp.concatenate(
            [jnp.broadcast_to(row(j * SUBLANES + r), (SUBLANES, HEAD_DIM))
             for j in range(SCAN_CHUNK // SUBLANES)], axis=0)
    if rows_per_group == SUBLANES:
        return pick(r0)
    return jnp.where(low_half, pick(r0), pick(r0 + rows_per_group))


def _level_operands(q, kf, kb, pf, cb, pf_row, cb_row, h, late_rows, low_half):
    if h >= SUBLANES:
        lhs, rhs = [], []
        for r0 in range(0, SCAN_CHUNK, 2 * h):
            early, late = slice(r0, r0 + h), slice(r0 + h, r0 + 2 * h)
            ref_f = pf_row(r0 + h - 1)
            ref_b = cb_row(r0 + h)
            lhs += [q[early] * jnp.exp2(cb[early] - ref_b), q[late] * jnp.exp2(pf[late] - ref_f)]
            rhs += [kf[early] * jnp.exp2(ref_f - pf[early]), kb[late] * jnp.exp2(ref_b - cb[late])]
        return jnp.concatenate(lhs, axis=0), jnp.concatenate(rhs, axis=0)
    late = late_rows[h]
    k_sel = jnp.where(late, kb, kf)
    if h == 1:
        d_f = pf - pltpu.roll(pf, 1, axis=0)
        d_b = cb - pltpu.roll(cb, SCAN_CHUNK - 1, axis=0)
        return q * jnp.exp2(jnp.where(late, d_f, d_b)), k_sel
    d_f = pf - _sublane_ref(pf_row, 2 * h, h - 1, low_half)
    d_b = cb - _sublane_ref(cb_row, 2 * h, h, low_half)
    lhs = q * jnp.exp2(jnp.where(late, d_f, d_b))
    rhs = k_sel * jnp.exp2(-jnp.where(late, d_b, d_f))
    return lhs, rhs


def _bwd_scan_kernel(kb_ref, cb_ref, v_ref, hist_ref, st_ref):
    heads = kb_ref.shape[0]
    nchunk = kb_ref.shape[1] // SCAN_CHUNK

    @pl.when(pl.program_id(1) == 0)
    def _():
        st_ref[...] = jnp.zeros_like(st_ref)

    def body(ci, carry):
        c = nchunk - 1 - ci
        rows = pl.ds(pl.multiple_of(c * SCAN_CHUNK, SCAN_CHUNK), SCAN_CHUNK)
        for h in range(heads):
            cb = cb_ref[h, rows, :]
            st = st_ref[h]
            hist_ref[h, c] = st.astype(hist_ref.dtype)
            tot = cb[0:1, :]
            kt = (kb_ref[h, rows, :].astype(F32) * jnp.exp2(tot - cb)).astype(BF16)
            st_ref[h] = st * jnp.exp2(tot) + _dot_tn(v_ref[h, rows, :], kt)
        return carry

    lax.fori_loop(0, nchunk, body, 0, unroll=8)


def _bwd_scan_call(kb, cb, v, *, tb):
    batch, heads, seq, _ = kb.shape
    nblk = seq // tb
    spec = pl.BlockSpec((None, heads, tb, HEAD_DIM), lambda b, i: (b, 0, nblk - 1 - i, 0))
    return pl.pallas_call(
        _bwd_scan_kernel,
        grid=(batch, nblk),
        in_specs=[spec, spec, spec],
        out_specs=pl.BlockSpec((None, heads, tb // SCAN_CHUNK, HEAD_DIM, HEAD_DIM),
                               lambda b, i: (b, 0, nblk - 1 - i, 0, 0)),
        out_shape=jax.ShapeDtypeStruct(
            (batch, heads, seq // SCAN_CHUNK, HEAD_DIM, HEAD_DIM), BF16),
        scratch_shapes=[pltpu.VMEM((heads, HEAD_DIM, HEAD_DIM), F32)],
        compiler_params=pltpu.CompilerParams(
            dimension_semantics=("arbitrary", "arbitrary"), vmem_limit_bytes=VMEM_LIMIT),
        name="hgrn_bwd_scan",
    )(kb, cb, v)


def _fwd_kernel(q_ref, kf_ref, kb_ref, pf_ref, cb_ref, v_ref, g_ref, hist_ref, og_ref,
                a_ref, st_ref, score_ref, part_ref):
    heads = q_ref.shape[0]
    nchunk = q_ref.shape[1] // SCAN_CHUNK

    @pl.when(pl.program_id(2) == 0)
    def _():
        st_ref[...] = jnp.zeros_like(st_ref)

    def chunk_rows(c):
        return pl.ds(pl.multiple_of(c * SCAN_CHUNK, SCAN_CHUNK), SCAN_CHUNK)

    zeros = jnp.zeros((SCAN_CHUNK, HEAD_DIM), BF16)

    def side_by_side(even, odd):
        return jnp.concatenate([even, odd], axis=1)

    def block_diag(even, odd):
        return jnp.concatenate([side_by_side(even, zeros), side_by_side(zeros, odd)], axis=0)

    def stage1(p, c, masks):
        rows, slot = chunk_rows(c), c % 2
        late_rows, low_half, _ = masks
        operands = []
        for h in (2 * p, 2 * p + 1):
            q = q_ref[h, rows, :].astype(F32)
            kf = kf_ref[h, rows, :].astype(F32)
            kb = kb_ref[h, rows, :].astype(F32)
            pf = pf_ref[h, rows, :]
            cb = cb_ref[h, rows, :]
            v = v_ref[h, rows, :]
            pf_row = lambda r, h=h: pf_ref[h, pl.ds(c * SCAN_CHUNK + r, 1), :]
            cb_row = lambda r, h=h: cb_ref[h, pl.ds(c * SCAN_CHUNK + r, 1), :]
            tiles = [_level_operands(q, kf, kb, pf, cb, pf_row, cb_row, lvl, late_rows, low_half)
                     for lvl in LEVELS]
            tiles.append((q, kf + kb))
            operands.append([(lhs.astype(BF16), rhs.astype(BF16)) for lhs, rhs in tiles])

            st = st_ref[h]
            q_in = jnp.concatenate([q * jnp.exp2(pf), q * jnp.exp2(cb)], axis=1)
            s_in = jnp.concatenate([st.astype(BF16), hist_ref[h, c]], axis=1)
            part_ref[slot, h] = _dot_nt(q_in.astype(BF16), s_in)

            tot = pf[SCAN_CHUNK - 1:SCAN_CHUNK, :]
            kt = (kf * jnp.exp2(tot - pf)).astype(BF16)
            st_ref[h] = st * jnp.exp2(tot) + _dot_tn(v, kt)
        for li, ((lhs_e, rhs_e), (lhs_o, rhs_o)) in enumerate(zip(*operands)):
            score_ref[slot, p, li] = _dot_nt(side_by_side(lhs_e, lhs_o), block_diag(rhs_e, rhs_o))

    def stage2(p, c, masks):
        rows, slot = chunk_rows(c), c % 2
        pair_level = masks[2]
        a = score_ref[slot, p, len(LEVELS)]
        for li, lvl in enumerate(LEVELS):
            a = jnp.where(pair_level[lvl], score_ref[slot, p, li], a)
        v_pair = block_diag(v_ref[2 * p, rows, :], v_ref[2 * p + 1, rows, :])
        o_pair = _dot(a.astype(BF16), v_pair)
        for j, h in enumerate((2 * p, 2 * p + 1)):
            o = o_pair[:, j * HEAD_DIM:(j + 1) * HEAD_DIM] + part_ref[slot, h]
            o = _rms(o, og_ref[...]) * g_ref[h, rows, :].astype(F32)
            a_ref[h, rows, :] = o.astype(a_ref.dtype)

    pairs = heads // 2
    masks0 = _chunk_masks()
    for p in range(pairs):
        stage1(p, 0, masks0)

    def body(c, carry):
        masks = _chunk_masks()
        for p in range(pairs):
            stage2(p, c - 1, masks)
        for p in range(pairs):
            stage1(p, c, masks)
        return carry

    lax.fori_loop(1, nchunk, body, 0, unroll=5)
    for p in range(pairs):
        stage2(p, nchunk - 1, masks0)


def _fwd_call(q, kf, kb, pf, cb, v, g, hist, og, *, tb, hp):
    batch, heads, seq, _ = q.shape
    spec = pl.BlockSpec((None, hp, tb, HEAD_DIM), lambda b, hg, i: (b, hg, i, 0))
    return pl.pallas_call(
        _fwd_kernel,
        grid=(batch, heads // hp, seq // tb),
        in_specs=[spec] * 7 + [
            pl.BlockSpec((None, hp, tb // SCAN_CHUNK, HEAD_DIM, HEAD_DIM),
                         lambda b, hg, i: (b, hg, i, 0, 0)),
            _resident(og.shape)],
        out_specs=spec,
        out_shape=jax.ShapeDtypeStruct((batch, heads, seq, HEAD_DIM), BF16),
        scratch_shapes=[pltpu.VMEM((hp, HEAD_DIM, HEAD_DIM), F32),
                        pltpu.VMEM((2, hp // 2, len(LEVELS) + 1, SCAN_CHUNK, 2 * SCAN_CHUNK), F32),
                        pltpu.VMEM((2, hp, SCAN_CHUNK, HEAD_DIM), F32)],
        compiler_params=pltpu.CompilerParams(
            dimension_semantics=("arbitrary", "arbitrary", "arbitrary"),
            vmem_limit_bytes=VMEM_LIMIT),
        name="hgrn_fwd",
    )(q, kf, kb, pf, cb, v, g, hist, og)


def _outproj_kernel(x_ref, a_ref, s_ref, wout_ref, gffn_ref, h_ref, hn_ref):
    mix = jnp.concatenate([a_ref[h] for h in range(HEADS)] + [s_ref[...]], axis=1)
    h = x_ref[...] + _dot(mix, _unpack(wout_ref[...]))
    h_ref[...] = h
    hn_ref[...] = _rms(h, gffn_ref[...]).astype(hn_ref.dtype)


def _outproj_call(x2, a, s, wout, gffn, *, seq, tm):
    tokens, d_model = x2.shape
    nlb = seq // tm
    tile = pl.BlockSpec((tm, d_model), lambda i: (i, 0))
    return pl.pallas_call(
        _outproj_kernel,
        grid=(tokens // tm,),
        in_specs=[tile,
                  pl.BlockSpec((None, HEADS, tm, HEAD_DIM), lambda i: (i // nlb, 0, i % nlb, 0)),
                  pl.BlockSpec((tm, s.shape[1]), lambda i: (i, 0)),
                  _resident(wout.shape), _resident(gffn.shape)],
        out_specs=[tile, tile],
        out_shape=[jax.ShapeDtypeStruct((tokens, d_model), F32),
                   jax.ShapeDtypeStruct((tokens, d_model), BF16)],
        compiler_params=pltpu.CompilerParams(
            dimension_semantics=("arbitrary",), vmem_limit_bytes=VMEM_LIMIT),
        name="outproj",
    )(x2, a, s, wout, gffn)


def _ffn_kernel(hn_ref, wg_hbm, wu_hbm, wd_hbm, o_ref, *, th):
    d_model = o_ref.shape[1]
    o_ref[...] = jnp.zeros_like(o_ref)

    def hidden_tile(wg_ref, wu_ref, wd_ref):
        hn = hn_ref[...]
        gate = _dot(hn, wg_ref[...])
        act = (gate * _sigmoid(gate) * _dot(hn, wu_ref[...])).astype(BF16)
        o_ref[...] += _dot(act, wd_ref[...])

    deep = pl.Buffered(3)
    pltpu.emit_pipeline(
        hidden_tile,
        grid=(wg_hbm.shape[1] // th,),
        in_specs=[pl.BlockSpec((d_model, th), lambda j: (0, j), pipeline_mode=deep),
                  pl.BlockSpec((d_model, th), lambda j: (0, j), pipeline_mode=deep),
                  pl.BlockSpec((th, d_model), lambda j: (j, 0), pipeline_mode=deep)],
    )(wg_hbm, wu_hbm, wd_hbm)


def _ffn_call(hn, wg, wu, wd, *, tm, th):
    tokens, d_model = hn.shape
    tile = pl.BlockSpec((tm, d_model), lambda i: (i, 0))
    hbm = pl.BlockSpec(memory_space=pl.ANY)
    return pl.pallas_call(
        functools.partial(_ffn_kernel, th=th),
        grid=(tokens // tm,),
        in_specs=[tile, hbm, hbm, hbm],
        out_specs=tile,
        out_shape=jax.ShapeDtypeStruct((tokens, d_model), F32),
        compiler_params=pltpu.CompilerParams(
            dimension_semantics=("arbitrary",), vmem_limit_bytes=VMEM_LIMIT),
        name="ffn",
    )(hn, wg, wu, wd)


def _ple_kernel(h_ref, f_ref, p_ref, gple_ref, wpg_ref, wpp_ref, gout_ref, o_ref):
    h2 = h_ref[...] + f_ref[...]
    hp = _rms(h2, gple_ref[...]).astype(BF16)
    pgate = _sigmoid(_dot(hp, _unpack(wpg_ref[...])))
    pproj = _dot(p_ref[...].astype(BF16), wpp_ref[...])
    o_ref[...] = _rms(h2 + pgate * pproj, gout_ref[...])


def _ple_call(h, f, p2, gple, wpg, wpp, gout, *, tm):
    tokens, d_model = h.shape
    tile = pl.BlockSpec((tm, d_model), lambda i: (i, 0))
    return pl.pallas_call(
        _ple_kernel,
        grid=(tokens // tm,),
        in_specs=[tile, tile, pl.BlockSpec((tm, p2.shape[1]), lambda i: (i, 0)),
                  _resident(gple.shape), _resident(wpg.shape), _resident(wpp.shape),
                  _resident(gout.shape)],
        out_specs=tile,
        out_shape=jax.ShapeDtypeStruct((tokens, d_model), F32),
        compiler_params=pltpu.CompilerParams(
            dimension_semantics=("arbitrary",), vmem_limit_bytes=VMEM_LIMIT),
        name="ple_final",
    )(h, f, p2, gple, wpg, wpp, gout)


def _layer(h2, p2, lb_f, lb_b, norm_mix_g, w_in, hgrn_onorm_g, sgu_ln_g, sgu_ln_b, sgu_w,
           sgu_b, sgu_onorm_g, w_out, norm_ffn_g, w_gate, w_up, w_down, norm_ple_g,
           w_ple_gate, w_ple_proj, out_g, *, batch, seq):
    row = lambda a: a.reshape(1, -1).astype(F32)
    q, kf, kb, pf, cb, v, g, s = _inproj_call(
        h2, row(norm_mix_g), _pack_weight_call(w_in, rows=256), row(lb_f), row(lb_b),
        row(sgu_ln_g),
        row(sgu_ln_b), sgu_w.astype(BF16), sgu_b.astype(F32)[:, :, None], row(sgu_onorm_g),
        batch=batch, seq=seq, tm=256)
    hist = _bwd_scan_call(kb, cb, v, tb=1024)
    a = _fwd_call(q, kf, kb, pf, cb, v, g, hist, row(hgrn_onorm_g), tb=1024, hp=8)
    hmid, hn = _outproj_call(h2, a, s, _pack_weight_call(w_out, rows=512), row(norm_ffn_g),
                             seq=seq, tm=512)
    ffn = _ffn_call(hn, w_gate.astype(BF16), w_up.astype(BF16), w_down.astype(BF16),
                    tm=1024, th=512)
    return _ple_call(hmid, ffn, p2, row(norm_ple_g), _pack_weight_call(w_ple_gate, rows=512),
                     w_ple_proj.astype(BF16), row(out_g), tm=512)


def kernel(x, p, norm_mix_g, w_in, lb_fwd_logits, lb_bwd_logits, hgrn_onorm_g, sgu_ln_g, sgu_ln_b, sgu_w, sgu_b, sgu_onorm_g, w_out, norm_ffn_g, w_gate, w_up, w_down, norm_ple_g, w_ple_gate, w_ple_proj, final_norm_g):
    batch, seq, d_model = x.shape
    depth = w_in.shape[0]
    lb_f_all = jnp.cumsum(jax.nn.softmax(lb_fwd_logits.astype(F32), axis=0), axis=0)
    lb_b_all = jnp.cumsum(jax.nn.softmax(lb_bwd_logits.astype(F32), axis=0), axis=0)
    h = x.reshape(batch * seq, d_model)
    for layer in range(depth):
        assert layer == depth - 1
        h = _layer(h, p[layer].reshape(batch * seq, -1), lb_f_all[layer], lb_b_all[layer],
                   norm_mix_g[layer], w_in[layer], hgrn_onorm_g[layer], sgu_ln_g[layer],
                   sgu_ln_b[layer], sgu_w[layer], sgu_b[layer], sgu_onorm_g[layer],
                   w_out[layer], norm_ffn_g[layer], w_gate[layer], w_up[layer],
                   w_down[layer], norm_ple_g[layer], w_ple_gate[layer], w_ple_proj[layer],
                   final_norm_g, batch=batch, seq=seq)
    return h.reshape(batch, seq, d_model)
```

```python
import jax
import jax.numpy as jnp
from jax import lax
from jax.experimental import pallas as pl
from jax.experimental.pallas import tpu as pltpu

F32 = jnp.float32
BF16 = jnp.bfloat16
EPS = 1e-6

HEADS = 8
HEAD_DIM = 128
SUBLANES = 8
SCAN_CHUNK = 64
SGU_CHUNK = 128
LEVELS = (1, 2, 4, 8, 16, 32)
GATE_COL_PARTS = 2
GATE_ROW_PARTS = 2
VMEM_LIMIT = 60 * 1024 * 1024


def _dot(a, b):
    return jnp.dot(a, b, preferred_element_type=F32)


def _dot_nt(a, b):
    return lax.dot_general(a, b, (((1,), (1,)), ((), ())), preferred_element_type=F32)


def _dot_tn(a, b):
    return lax.dot_general(a, b, (((0,), (0,)), ((), ())), preferred_element_type=F32)


def _sigmoid(x):
    return 1.0 / (1.0 + jnp.exp(-x))


def _rms(x, g):
    return x * lax.rsqrt(jnp.mean(x * x, axis=-1, keepdims=True) + EPS) * g


def _unpack(w_ref_or_val):
    return pltpu.bitcast(w_ref_or_val, BF16)


def _pack_kernel(w_ref, o_ref):
    o_ref[...] = pltpu.bitcast(w_ref[...].astype(BF16), jnp.uint32)


def _pack_weight_call(w, *, rows):
    k, n = w.shape
    return pl.pallas_call(
        _pack_kernel,
        grid=(k // rows,),
        in_specs=[pl.BlockSpec((rows, n), lambda i: (i, 0))],
        out_specs=pl.BlockSpec((rows // 2, n), lambda i: (i, 0)),
        out_shape=jax.ShapeDtypeStruct((k // 2, n), jnp.uint32),
        compiler_params=pltpu.CompilerParams(
            dimension_semantics=("arbitrary",), vmem_limit_bytes=VMEM_LIMIT),
        name="pack_weight",
    )(w)


def _resident(shape):
    nd = len(shape)
    return pl.BlockSpec(shape, lambda *_: (0,) * nd, pipeline_mode=pl.Buffered(1))


def _chunk_cumsum(x, reverse):
    rows, cols = x.shape
    x3 = x.reshape(rows // SUBLANES, SUBLANES, cols)
    sub = lax.broadcasted_iota(jnp.int32, x3.shape, 1)
    s = 1
    while s < SUBLANES:
        if reverse:
            x3 = x3 + jnp.where(sub < SUBLANES - s, pltpu.roll(x3, SUBLANES - s, axis=1), 0.0)
        else:
            x3 = x3 + jnp.where(sub >= s, pltpu.roll(x3, s, axis=1), 0.0)
        s *= 2
    groups = SCAN_CHUNK // SUBLANES
    edge = 0 if reverse else SUBLANES - 1
    out = [None] * (rows // SUBLANES)
    for c in range(rows // SCAN_CHUNK):
        carry = None
        order = range(groups - 1, -1, -1) if reverse else range(groups)
        for j in order:
            blk = x3[c * groups + j]
            total = blk[edge:edge + 1, :]
            out[c * groups + j] = blk if carry is None else blk + carry
            carry = total if carry is None else carry + total
    return jnp.concatenate(out, axis=0)


def _sgu(u_raw, v_raw, lng, lnb, sw_ref, sb_ref, sog, s_ref):
    def gelu(t):
        return 0.5 * t * (1.0 + lax.erf(t * (2.0 ** -0.5)))

    v = gelu(v_raw)
    vc = v - jnp.mean(v, axis=-1, keepdims=True)
    v = vc * lax.rsqrt(jnp.mean(vc * vc, axis=-1, keepdims=True) + EPS)
    v = (v * lng + lnb).astype(BF16)
    u = gelu(u_raw)
    for c in range(u.shape[0] // SGU_CHUNK):
        rows = slice(c * SGU_CHUNK, (c + 1) * SGU_CHUNK)
        mixed = [
            _dot(sw_ref[gi], v[rows, gi * HEAD_DIM:(gi + 1) * HEAD_DIM]) + sb_ref[gi]
            for gi in range(HEADS)
        ]
        s_ref[rows, :] = _rms(u[rows] * jnp.concatenate(mixed, axis=1), sog).astype(s_ref.dtype)


def _inproj_kernel(x_ref, gmix_ref, win_ref, lbf_ref, lbb_ref, lng_ref, lnb_ref,
                   sw_ref, sb_ref, sog_ref,
                   q_ref, kf_ref, kb_ref, pf_ref, cb_ref, v_ref, g_ref, s_ref):
    width = HEADS * HEAD_DIM
    xb = _rms(x_ref[...], gmix_ref[...]).astype(BF16)

    def proj(j):
        return _dot(xb, _unpack(win_ref[:, j * width:(j + 1) * width]))

    def put_heads(ref, val):
        for h in range(HEADS):
            ref[h] = val[:, h * HEAD_DIM:(h + 1) * HEAD_DIM].astype(ref.dtype)

    v_raw = proj(6)
    _sgu(proj(5), v_raw, lng_ref[...], lnb_ref[...], sw_ref, sb_ref, sog_ref[...], s_ref)

    def gates(j, lb_ref, k_ref, cum_ref, reverse):
        piece_w = width // GATE_COL_PARTS
        piece_h = xb.shape[0] // GATE_ROW_PARTS
        heads_per_piece = piece_w // HEAD_DIM
        for part in range(GATE_COL_PARTS):
            lb = lb_ref[:, part * piece_w:(part + 1) * piece_w]
            col = j * width + part * piece_w
            w = _unpack(win_ref[:, col:col + piece_w])
            for r0 in range(0, xb.shape[0], piece_h):
                rows = slice(r0, r0 + piece_h)
                sig = _sigmoid(_dot(xb[rows], w))
                cum = _chunk_cumsum(jnp.log2(lb + (1.0 - lb) * sig), reverse)
                k = (1.0 - lb) * (1.0 - sig)
                for h in range(heads_per_piece):
                    hh = part * heads_per_piece + h
                    cum_ref[hh, rows, :] = cum[:, h * HEAD_DIM:(h + 1) * HEAD_DIM]
                    k_ref[hh, rows, :] = k[:, h * HEAD_DIM:(h + 1) * HEAD_DIM].astype(k_ref.dtype)

    gates(1, lbf_ref, kf_ref, pf_ref, False)
    q = proj(0)
    put_heads(q_ref, q * _sigmoid(q))
    gates(2, lbb_ref, kb_ref, cb_ref, True)
    g = proj(4)
    put_heads(g_ref, g * _sigmoid(g))
    put_heads(v_ref, proj(3))


def _inproj_call(x2, gmix, win, lbf, lbb, lng, lnb, sw, sb, sog, *, batch, seq, tm):
    tokens, d_model = x2.shape
    width = HEADS * HEAD_DIM
    nlb = seq // tm
    row = lambda a: _resident(a.shape)
    head_spec = pl.BlockSpec((None, HEADS, tm, HEAD_DIM), lambda i: (i // nlb, 0, i % nlb, 0))
    head_shape = lambda dt: jax.ShapeDtypeStruct((batch, HEADS, seq, HEAD_DIM), dt)
    return pl.pallas_call(
        _inproj_kernel,
        grid=(tokens // tm,),
        in_specs=[pl.BlockSpec((tm, d_model), lambda i: (i, 0)),
                  row(gmix), row(win), row(lbf), row(lbb), row(lng), row(lnb),
                  row(sw), row(sb), row(sog)],
        out_specs=[head_spec] * 7 + [pl.BlockSpec((tm, width), lambda i: (i, 0))],
        out_shape=[head_shape(BF16), head_shape(BF16), head_shape(BF16),
                   head_shape(F32), head_shape(F32), head_shape(BF16), head_shape(BF16),
                   jax.ShapeDtypeStruct((tokens, width), BF16)],
        compiler_params=pltpu.CompilerParams(
            dimension_semantics=("arbitrary",), vmem_limit_bytes=VMEM_LIMIT),
        name="inproj",
    )(x2, gmix, win, lbf, lbb, lng, lnb, sw, sb, sog)


def _chunk_masks():
    sub = lax.broadcasted_iota(jnp.int32, (SCAN_CHUNK, HEAD_DIM), 0) & (SUBLANES - 1)
    pair_xor = (lax.broadcasted_iota(jnp.int32, (SCAN_CHUNK, 2 * SCAN_CHUNK), 0)
                ^ (lax.broadcasted_iota(jnp.int32, (SCAN_CHUNK, 2 * SCAN_CHUNK), 1)
                   & (SCAN_CHUNK - 1)))
    late = {h: (sub & h) != 0 for h in LEVELS if h < SUBLANES}
    return late, sub < SUBLANES // 2, {h: pair_xor >= h for h in LEVELS}


def _sublane_ref(row, rows_per_group, r0, low_half):
    def pick(r):
        return jnp.concatenate(
            [jnp.broadcast_to(row(j * SUBLANES + r), (SUBLANES, HEAD_DIM))
             for j in range(SCAN_CHUNK // SUBLANES)], axis=0)
    if rows_per_group == SUBLANES:
        return pick(r0)
    return jnp.where(low_half, pick(r0), pick(r0 + rows_per_group))


def _level_operands(q, kf, kb, pf, cb, pf_row, cb_row, h, late_rows, low_half):
    if h >= SUBLANES:
        lhs, rhs = [], []
        for r0 in range(0, SCAN_CHUNK, 2 * h):
            early, late = slice(r0, r0 + h), slice(r0 + h, r0 + 2 * h)
            ref_f = pf_row(r0 + h - 1)
            ref_b = cb_row(r0 + h)
            lhs += [q[early] * jnp.exp2(cb[early] - ref_b), q[late] * jnp.exp2(pf[late] - ref_f)]
            rhs += [kf[early] * jnp.exp2(ref_f - pf[early]), kb[late] * jnp.exp2(ref_b - cb[late])]
        return jnp.concatenate(lhs, axis=0), jnp.concatenate(rhs, axis=0)
    late = late_rows[h]
    k_sel = jnp.where(late, kb, kf)
    if h == 1:
        d_f = pf - pltpu.roll(pf, 1, axis=0)
        d_b = cb - pltpu.roll(cb, SCAN_CHUNK - 1, axis=0)
        return q * jnp.exp2(jnp.where(late, d_f, d_b)), k_sel
    d_f = pf - _sublane_ref(pf_row, 2 * h, h - 1, low_half)
    d_b = cb - _sublane_ref(cb_row, 2 * h, h, low_half)
    lhs = q * jnp.exp2(jnp.where(late, d_f, d_b))
    rhs = k_sel * jnp.exp2(-jnp.where(late, d_b, d_f))
    return lhs, rhs


def _bwd_scan_kernel(kb_ref, cb_ref, v_ref, hist_ref, st_ref):
    heads = kb_ref.shape[0]
    nchunk = kb_ref.shape[1] // SCAN_CHUNK

    @pl.when(pl.program_id(1) == 0)
    def _():
        st_ref[...] = jnp.zeros_like(st_ref)

    def body(ci, carry):
        c = nchunk - 1 - ci
        rows = pl.ds(pl.multiple_of(c * SCAN_CHUNK, SCAN_CHUNK), SCAN_CHUNK)
        for h in range(heads):
            cb = cb_ref[h, rows, :]
            st = st_ref[h]
            hist_ref[h, c] = st.astype(hist_ref.dtype)
            tot = cb[0:1, :]
            kt = (kb_ref[h, rows, :].astype(F32) * jnp.exp2(tot - cb)).astype(BF16)
            st_ref[h] = st * jnp.exp2(tot) + _dot_tn(v_ref[h, rows, :], kt)
        return carry

    lax.fori_loop(0, nchunk, body, 0, unroll=8)


def _bwd_scan_call(kb, cb, v, *, tb):
    batch, heads, seq, _ = kb.shape
    nblk = seq // tb
    spec = pl.BlockSpec((None, heads, tb, HEAD_DIM), lambda b, i: (b, 0, nblk - 1 - i, 0))
    return pl.pallas_call(
        _bwd_scan_kernel,
        grid=(batch, nblk),
        in_specs=[spec, spec, spec],
        out_specs=pl.BlockSpec((None, heads, tb // SCAN_CHUNK, HEAD_DIM, HEAD_DIM),
                               lambda b, i: (b, 0, nblk - 1 - i, 0, 0)),
        out_shape=jax.ShapeDtypeStruct(
            (batch, heads, seq // SCAN_CHUNK, HEAD_DIM, HEAD_DIM), BF16),
        scratch_shapes=[pltpu.VMEM((heads, HEAD_DIM, HEAD_DIM), F32)],
        compiler_params=pltpu.CompilerParams(
            dimension_semantics=("arbitrary", "arbitrary"), vmem_limit_bytes=VMEM_LIMIT),
        name="hgrn_bwd_scan",
    )(kb, cb, v)


def _fwd_kernel(q_ref, kf_ref, kb_ref, pf_ref, cb_ref, v_ref, g_ref, hist_ref, og_ref,
                a_ref, st_ref, score_ref, part_ref):
    heads = q_ref.shape[0]
    nchunk = q_ref.shape[1] // SCAN_CHUNK

    @pl.when(pl.program_id(2) == 0)
    def _():
        st_ref[...] = jnp.zeros_like(st_ref)

    def chunk_rows(c):
        return pl.ds(pl.multiple_of(c * SCAN_CHUNK, SCAN_CHUNK), SCAN_CHUNK)

    zeros = jnp.zeros((SCAN_CHUNK, HEAD_DIM), BF16)

    def side_by_side(even, odd):
        return jnp.concatenate([even, odd], axis=1)

    def block_diag(even, odd):
        return jnp.concatenate([side_by_side(even, zeros), side_by_side(zeros, odd)], axis=0)

    def stage1(p, c, masks):
        rows, slot = chunk_rows(c), c % 2
        late_rows, low_half, _ = masks
        operands = []
        for h in (2 * p, 2 * p + 1):
            q = q_ref[h, rows, :].astype(F32)
            kf = kf_ref[h, rows, :].astype(F32)
            kb = kb_ref[h, rows, :].astype(F32)
            pf = pf_ref[h, rows, :]
            cb = cb_ref[h, rows, :]
            v = v_ref[h, rows, :]
            pf_row = lambda r, h=h: pf_ref[h, pl.ds(c * SCAN_CHUNK + r, 1), :]
            cb_row = lambda r, h=h: cb_ref[h, pl.ds(c * SCAN_CHUNK + r, 1), :]
            tiles = [_level_operands(q, kf, kb, pf, cb, pf_row, cb_row, lvl, late_rows, low_half)
                     for lvl in LEVELS]
            tiles.append((q, kf + kb))
            operands.append([(lhs.astype(BF16), rhs.astype(BF16)) for lhs, rhs in tiles])

            st = st_ref[h]
            q_in = jnp.concatenate([q * jnp.exp2(pf), q * jnp.exp2(cb)], axis=1)
            s_in = jnp.concatenate([st.astype(BF16), hist_ref[h, c]], axis=1)
            part_ref[slot, h] = _dot_nt(q_in.astype(BF16), s_in)

            tot = pf[SCAN_CHUNK - 1:SCAN_CHUNK, :]
            kt = (kf * jnp.exp2(tot - pf)).astype(BF16)
            st_ref[h] = st * jnp.exp2(tot) + _dot_tn(v, kt)
        for li, ((lhs_e, rhs_e), (lhs_o, rhs_o)) in enumerate(zip(*operands)):
            score_ref[slot, p, li] = _dot_nt(side_by_side(lhs_e, lhs_o), block_diag(rhs_e, rhs_o))

    def stage2(p, c, masks):
        rows, slot = chunk_rows(c), c % 2
        pair_level = masks[2]
        a = score_ref[slot, p, len(LEVELS)]
        for li, lvl in enumerate(LEVELS):
            a = jnp.where(pair_level[lvl], score_ref[slot, p, li], a)
        v_pair = block_diag(v_ref[2 * p, rows, :], v_ref[2 * p + 1, rows, :])
        o_pair = _dot(a.astype(BF16), v_pair)
        for j, h in enumerate((2 * p, 2 * p + 1)):
            o = o_pair[:, j * HEAD_DIM:(j + 1) * HEAD_DIM] + part_ref[slot, h]
            o = _rms(o, og_ref[...]) * g_ref[h, rows, :].astype(F32)
            a_ref[h, rows, :] = o.astype(a_ref.dtype)

    pairs = heads // 2
    masks0 = _chunk_masks()
    for p in range(pairs):
        stage1(p, 0, masks0)

    def body(c, carry):
        masks = _chunk_masks()
        for p in range(pairs):
            stage2(p, c - 1, masks)
        for p in range(pairs):
            stage1(p, c, masks)
        return carry

    lax.fori_loop(1, nchunk, body, 0, unroll=5)
    for p in range(pairs):
        stage2(p, nchunk - 1, masks0)


def _fwd_call(q, kf, kb, pf, cb, v, g, hist, og, *, tb, hp):
    batch, heads, seq, _ = q.shape
    spec = pl.BlockSpec((None, hp, tb, HEAD_DIM), lambda b, hg, i: (b, hg, i, 0))
    return pl.pallas_call(
        _fwd_kernel,
        grid=(batch, heads // hp, seq // tb),
        in_specs=[spec] * 7 + [
            pl.BlockSpec((None, hp, tb // SCAN_CHUNK, HEAD_DIM, HEAD_DIM),
                         lambda b, hg, i: (b, hg, i, 0, 0)),
            _resident(og.shape)],
        out_specs=spec,
        out_shape=jax.ShapeDtypeStruct((batch, heads, seq, HEAD_DIM), BF16),
        scratch_shapes=[pltpu.VMEM((hp, HEAD_DIM, HEAD_DIM), F32),
                        pltpu.VMEM((2, hp // 2, len(LEVELS) + 1, SCAN_CHUNK, 2 * SCAN_CHUNK), F32),
                        pltpu.VMEM((2, hp, SCAN_CHUNK, HEAD_DIM), F32)],
        compiler_params=pltpu.CompilerParams(
            dimension_semantics=("arbitrary", "arbitrary", "arbitrary"),
            vmem_limit_bytes=VMEM_LIMIT),
        name="hgrn_fwd",
    )(q, kf, kb, pf, cb, v, g, hist, og)


def _outproj_kernel(x_ref, a_ref, s_ref, wout_ref, gffn_ref, h_ref, hn_ref):
    w = _unpack(wout_ref[...])
    half = x_ref.shape[0] // 2
    for r0 in (0, half):
        rows = slice(r0, r0 + half)
        mix = jnp.concatenate([a_ref[h, rows, :] for h in range(HEADS)] + [s_ref[rows, :]], axis=1)
        h = x_ref[rows, :] + _dot(mix, w)
        h_ref[rows, :] = h
        hn_ref[rows, :] = _rms(h, gffn_ref[...]).astype(hn_ref.dtype)


def _outproj_call(x2, a, s, wout, gffn, *, seq, tm):
    tokens, d_model = x2.shape
    nlb = seq // tm
    tile = pl.BlockSpec((tm, d_model), lambda i: (i, 0))
    return pl.pallas_call(
        _outproj_kernel,
        grid=(tokens // tm,),
        in_specs=[tile,
                  pl.BlockSpec((None, HEADS, tm, HEAD_DIM), lambda i: (i // nlb, 0, i % nlb, 0)),
                  pl.BlockSpec((tm, s.shape[1]), lambda i: (i, 0)),
                  _resident(wout.shape), _resident(gffn.shape)],
        out_specs=[tile, tile],
        out_shape=[jax.ShapeDtypeStruct((tokens, d_model), F32),
                   jax.ShapeDtypeStruct((tokens, d_model), BF16)],
        compiler_params=pltpu.CompilerParams(
            dimension_semantics=("arbitrary",), vmem_limit_bytes=VMEM_LIMIT),
        name="outproj",
    )(x2, a, s, wout, gffn)


def _ffn_kernel(hn_ref, wg_ref, wu_ref, wd_ref, o_ref):
    @pl.when(pl.program_id(1) == 0)
    def _():
        o_ref[...] = jnp.zeros_like(o_ref)

    hn = hn_ref[...]
    gate = _dot(hn, wg_ref[...])
    act = (gate * _sigmoid(gate) * _dot(hn, wu_ref[...])).astype(BF16)
    o_ref[...] += _dot(act, wd_ref[...])


def _ffn_call(hn, wg, wu, wd, *, tm, th):
    tokens, d_model = hn.shape
    hidden = wg.shape[1]
    tile = pl.BlockSpec((tm, d_model), lambda i, j: (i, 0))
    return pl.pallas_call(
        _ffn_kernel,
        grid=(tokens // tm, hidden // th),
        in_specs=[tile,
                  pl.BlockSpec((d_model, th), lambda i, j: (0, j)),
                  pl.BlockSpec((d_model, th), lambda i, j: (0, j)),
                  pl.BlockSpec((th, d_model), lambda i, j: (j, 0))],
        out_specs=tile,
        out_shape=jax.ShapeDtypeStruct((tokens, d_model), F32),
        compiler_params=pltpu.CompilerParams(
            dimension_semantics=("arbitrary", "arbitrary"), vmem_limit_bytes=VMEM_LIMIT),
        name="ffn",
    )(hn, wg, wu, wd)


def _ple_kernel(h_ref, f_ref, p_ref, gple_ref, wpg_ref, wpp_ref, gout_ref, o_ref):
    h2 = h_ref[...] + f_ref[...]
    hp = _rms(h2, gple_ref[...]).astype(BF16)
    pgate = _sigmoid(_dot(hp, _unpack(wpg_ref[...])))
    pproj = _dot(p_ref[...].astype(BF16), wpp_ref[...])
    o_ref[...] = _rms(h2 + pgate * pproj, gout_ref[...])


def _ple_call(h, f, p2, gple, wpg, wpp, gout, *, tm):
    tokens, d_model = h.shape
    tile = pl.BlockSpec((tm, d_model), lambda i: (i, 0))
    return pl.pallas_call(
        _ple_kernel,
        grid=(tokens // tm,),
        in_specs=[tile, tile, pl.BlockSpec((tm, p2.shape[1]), lambda i: (i, 0)),
                  _resident(gple.shape), _resident(wpg.shape), _resident(wpp.shape),
                  _resident(gout.shape)],
        out_specs=tile,
        out_shape=jax.ShapeDtypeStruct((tokens, d_model), F32),
        compiler_params=pltpu.CompilerParams(
            dimension_semantics=("arbitrary",), vmem_limit_bytes=VMEM_LIMIT),
        name="ple_final",
    )(h, f, p2, gple, wpg, wpp, gout)


def _layer(h2, p2, lb_f, lb_b, norm_mix_g, w_in, hgrn_onorm_g, sgu_ln_g, sgu_ln_b, sgu_w,
           sgu_b, sgu_onorm_g, w_out, norm_ffn_g, w_gate, w_up, w_down, norm_ple_g,
           w_ple_gate, w_ple_proj, out_g, *, batch, seq):
    row = lambda a: a.reshape(1, -1).astype(F32)
    q, kf, kb, pf, cb, v, g, s = _inproj_call(
        h2, row(norm_mix_g), _pack_weight_call(w_in, rows=256), row(lb_f), row(lb_b),
        row(sgu_ln_g),
        row(sgu_ln_b), sgu_w.astype(BF16), sgu_b.astype(F32)[:, :, None], row(sgu_onorm_g),
        batch=batch, seq=seq, tm=256)
    hist = _bwd_scan_call(kb, cb, v, tb=1024)
    a = _fwd_call(q, kf, kb, pf, cb, v, g, hist, row(hgrn_onorm_g), tb=1024, hp=8)
    hmid, hn = _outproj_call(h2, a, s, _pack_weight_call(w_out, rows=512), row(norm_ffn_g),
                             seq=seq, tm=512)
    ffn = _ffn_call(hn, w_gate.astype(BF16), w_up.astype(BF16), w_down.astype(BF16),
                    tm=1024, th=512)
    return _ple_call(hmid, ffn, p2, row(norm_ple_g), _pack_weight_call(w_ple_gate, rows=512),
                     w_ple_proj.astype(BF16), row(out_g), tm=512)


def kernel(x, p, norm_mix_g, w_in, lb_fwd_logits, lb_bwd_logits, hgrn_onorm_g, sgu_ln_g, sgu_ln_b, sgu_w, sgu_b, sgu_onorm_g, w_out, norm_ffn_g, w_gate, w_up, w_down, norm_ple_g, w_ple_gate, w_ple_proj, final_norm_g):
    batch, seq, d_model = x.shape
    depth = w_in.shape[0]
    lb_f_all = jnp.cumsum(jax.nn.softmax(lb_fwd_logits.astype(F32), axis=0), axis=0)
    lb_b_all = jnp.cumsum(jax.nn.softmax(lb_bwd_logits.astype(F32), axis=0), axis=0)
    h = x.reshape(batch * seq, d_model)
    for layer in range(depth):
        assert layer == depth - 1
        h = _layer(h, p[layer].reshape(batch * seq, -1), lb_f_all[layer], lb_b_all[layer],
                   norm_mix_g[layer], w_in[layer], hgrn_onorm_g[layer], sgu_ln_g[layer],
                   sgu_ln_b[layer], sgu_w[layer], sgu_b[layer], sgu_onorm_g[layer],
                   w_out[layer], norm_ffn_g[layer], w_gate[layer], w_up[layer],
                   w_down[layer], norm_ple_g[layer], w_ple_gate[layer], w_ple_proj[layer],
                   final_norm_g, batch=batch, seq=seq)
    return h.reshape(batch, seq, d_model)
```
